```python
import math
import jax, jax.numpy as jnp
from jax import lax
import numpy as np

D_MODEL = 4096
BATCH = 1
SEQ = 16384
DEPTH = 2
DEC_BATCH = 8
DEC_SEQ = 64
PAST_LEN = 2048

CHUNK = 64
EPS = 1e-6
RET_HEADS = 8
RET_DK = 64
RET_DV = 128
RET_QK = RET_HEADS * RET_DK
RET_WIDTH = RET_HEADS * RET_DV
ROPE_BASE = 10000.0
GDN_HEADS = 8
GDN_DK = 128
GDN_DV = 128
GDN_WIDTH = GDN_HEADS * GDN_DV
CONV_W = 4
GDN_CONV_CH = GDN_HEADS * (2 * GDN_DK + GDN_DV)
S5_WIDTH = D_MODEL - RET_WIDTH - GDN_WIDTH
S5_GROUP_CH = 16
S5_GROUPS = S5_WIDTH // S5_GROUP_CH
S5_STATE = 64
MIX_WIDTH = RET_WIDTH + GDN_WIDTH + S5_WIDTH
IN_SIZES = (RET_QK, RET_QK, RET_WIDTH, RET_WIDTH, GDN_CONV_CH, GDN_WIDTH, GDN_HEADS, GDN_HEADS, S5_WIDTH)
IN_COLS = 2 * RET_QK + 2 * RET_WIDTH + GDN_CONV_CH + GDN_WIDTH + 2 * GDN_HEADS + S5_WIDTH
PEER_HEADS = 8
PEER_NKEYS = 128
PEER_EXPERTS = PEER_NKEYS * PEER_NKEYS
PEER_DQ = 256
PEER_HALF = PEER_DQ // 2
PEER_TOPK = 16
PEER_BLOCK = 64

kernel_name = 'hybrid_stream_ret_gdn_s5_peer_step'

F32 = jnp.float32


def rmsnorm(x, g):
    xf = x.astype(F32)
    y = xf * lax.rsqrt(jnp.mean(xf * xf, -1, keepdims=True) + EPS)
    return (y * g.astype(F32)).astype(x.dtype)


def l2norm(x):
    return x * lax.rsqrt(jnp.sum(x * x, -1, keepdims=True) + EPS)


def rope(x, pos):
    half = x.shape[-1] // 2
    inv = ROPE_BASE ** (-jnp.arange(half, dtype=F32) / half)
    ang = pos.astype(F32)[:, None] * inv[None, :]
    cos = jnp.cos(ang)[None, :, None, :]
    sin = jnp.sin(ang)[None, :, None, :]
    x1, x2 = x[..., :half], x[..., half:]
    return jnp.concatenate([x1 * cos - x2 * sin, x1 * sin + x2 * cos], -1)


def run_chunks(step, s0, xs):
    t = xs[0].shape[1]
    if t <= CHUNK:
        return step(s0, *xs)
    n = t // CHUNK
    blk = lambda a: jnp.moveaxis(a.reshape(a.shape[0], n, CHUNK, *a.shape[2:]), 1, 0)
    s, ys = lax.scan(lambda s, xc: step(s, *xc), s0, tuple(blk(a) for a in xs))
    y = jnp.moveaxis(ys, 0, 1)
    return s, y.reshape(y.shape[0], t, *y.shape[3:])


def retention_chunk(s, q, k, v):
    c = q.shape[1]
    ld = jnp.log(1.0 - 2.0 ** (-5.0 - jnp.arange(RET_HEADS, dtype=F32)))
    pos = jnp.arange(c, dtype=F32)
    diff = pos[:, None] - pos[None, :]
    causal = diff >= 0
    dmat = jnp.where(causal, jnp.exp(ld[:, None, None] * jnp.where(causal, diff, 0.0)), 0.0)
    scores = jnp.einsum('bihd,bjhd->bhij', q, k) * dmat
    o = jnp.einsum('bhij,bjhe->bihe', scores, v)
    o = o + jnp.einsum('bihd,bhde->bihe', q, s) * jnp.exp(ld[None, :] * (pos[:, None] + 1.0))[None, :, :, None]
    kd = k * jnp.exp(ld[None, :] * (c - 1.0 - pos[:, None]))[None, :, :, None]
    s_new = s * jnp.exp(ld * c)[None, :, None, None] + jnp.einsum('bjhd,bjhe->bhde', kd, v)
    return s_new, o


def gdn_chunk(s, q, k, v, g, beta):
    c = q.shape[1]
    q, k, v = (jnp.swapaxes(a, 1, 2) for a in (q, k, v))
    g = jnp.swapaxes(g, 1, 2)
    beta = jnp.swapaxes(beta, 1, 2)
    gc = jnp.cumsum(g, -1)
    ii = jnp.arange(c)
    lower = ii[:, None] >= ii[None, :]
    strict = ii[:, None] > ii[None, :]
    diff = gc[..., :, None] - gc[..., None, :]
    decay = jnp.where(lower, jnp.exp(jnp.where(lower, diff, 0.0)), 0.0)
    kb = k * beta[..., None]
    a_mat = jnp.where(strict, jnp.einsum('bhid,bhjd->bhij', kb, k) * decay, 0.0)
    t_mat = a_mat + jnp.eye(c, dtype=a_mat.dtype)
    rhs = jnp.concatenate([v * beta[..., None], kb * jnp.exp(gc)[..., None]], -1)
    sol = lax.linalg.triangular_solve(t_mat, rhs, left_side=True, lower=True, unit_diagonal=True)
    u, w = sol[..., :GDN_DV], sol[..., GDN_DV:]
    v_new = u - jnp.einsum('bhcd,bhde->bhce', w, s)
    attn = jnp.einsum('bhid,bhjd->bhij', q, k) * decay
    o = jnp.einsum('bhcd,bhde->bhce', q * jnp.exp(gc)[..., None], s) + jnp.einsum('bhij,bhje->bhie', attn, v_new)
    g_last = gc[..., -1]
    kd = k * jnp.exp(g_last[..., None] - gc)[..., None]
    s_new = s * jnp.exp(g_last)[..., None, None] + jnp.einsum('bhcd,bhce->bhde', kd, v_new)
    return s_new, jnp.swapaxes(o, 1, 2)


def retention_mixer(q, k, v, gate, pos, norm_g, s0):
    b, t = q.shape[:2]
    q = rope(q.reshape(b, t, RET_HEADS, RET_DK), pos)
    k = rope(k.reshape(b, t, RET_HEADS, RET_DK), pos) * (RET_DK ** -0.5)
    v = v.reshape(b, t, RET_HEADS, RET_DV)
    s, o = run_chunks(retention_chunk, s0, (q, k, v))
    mu = jnp.mean(o, -1, keepdims=True)
    var = jnp.mean(jnp.square(o - mu), -1, keepdims=True)
    o = (o - mu) * lax.rsqrt(var + EPS) * norm_g.astype(F32).reshape(RET_HEADS, RET_DV)
    o = o * jax.nn.silu(gate.reshape(b, t, RET_HEADS, RET_DV))
    return o.reshape(b, t, RET_WIDTH), s


def gdn_mixer(qkv, gate, a, bta, conv_hist, conv_w, a_log, dt_bias, norm_g, s0):
    b, t = qkv.shape[:2]
    xin = jnp.concatenate([conv_hist.astype(F32), qkv], 1)
    y = lax.conv_general_dilated(xin, conv_w.astype(F32)[:, None, :], (1,), 'VALID',
                                 dimension_numbers=('NWC', 'WIO', 'NWC'),
                                 feature_group_count=GDN_CONV_CH)
    y = jax.nn.silu(y)
    new_hist = xin[:, -(CONV_W - 1):]
    q, k, v = jnp.split(y, [GDN_HEADS * GDN_DK, 2 * GDN_HEADS * GDN_DK], -1)
    q = l2norm(q.reshape(b, t, GDN_HEADS, GDN_DK)) * (GDN_DK ** -0.5)
    k = l2norm(k.reshape(b, t, GDN_HEADS, GDN_DK))
    v = v.reshape(b, t, GDN_HEADS, GDN_DV)
    g = -jnp.exp(a_log.astype(F32)) * jax.nn.softplus(a + dt_bias.astype(F32))
    beta = jax.nn.sigmoid(bta)
    s, o = run_chunks(gdn_chunk, s0, (q, k, v, g, beta))
    o = rmsnorm(o, norm_g) * jax.nn.silu(gate.reshape(b, t, GDN_HEADS, GDN_DV))
    return o.reshape(b, t, GDN_WIDTH), s, new_hist


def s5_mixer(u, x0_re, x0_im, lam_re, lam_im, log_dt, b_re, b_im, c_re, c_im, d_skip, glu_w, glu_b):
    bsz, t, _ = u.shape
    lam_re, lam_im, b_re, b_im, c_re, c_im = (p.astype(F32) for p in (lam_re, lam_im, b_re, b_im, c_re, c_im))
    ug = u.reshape(bsz, t, S5_GROUPS, S5_GROUP_CH)
    dt = jnp.exp(log_dt.astype(F32))[:, None]
    mag = jnp.exp(lam_re * dt)
    a_re = mag * jnp.cos(lam_im * dt)
    a_im = mag * jnp.sin(lam_im * dt)
    den = lam_re * lam_re + lam_im * lam_im
    f_re = ((a_re - 1.0) * lam_re + a_im * lam_im) / den
    f_im = (a_im * lam_re - (a_re - 1.0) * lam_im) / den
    bb_re = f_re[..., None] * b_re - f_im[..., None] * b_im
    bb_im = f_re[..., None] * b_im + f_im[..., None] * b_re
    e_re = jnp.einsum('btgn,gpn->btgp', ug, bb_re)
    e_im = jnp.einsum('btgn,gpn->btgp', ug, bb_im)
    x0_re = x0_re.astype(F32)
    x0_im = x0_im.astype(F32)
    e_re = e_re.at[:, 0].add(a_re * x0_re - a_im * x0_im)
    e_im = e_im.at[:, 0].add(a_re * x0_im + a_im * x0_re)
    ar = jnp.broadcast_to(a_re, e_re.shape)
    ai = jnp.broadcast_to(a_im, e_im.shape)

    def combine(l, r):
        lar, lai, lbr, lbi = l
        rar, rai, rbr, rbi = r
        return (lar * rar - lai * rai, lar * rai + lai * rar,
                rar * lbr - rai * lbi + rbr, rar * lbi + rai * lbr + rbi)

    _, _, xr, xi = lax.associative_scan(combine, (ar, ai, e_re, e_im), axis=1)
    y = jnp.einsum('btgp,gnp->btgn', xr, c_re) - jnp.einsum('btgp,gnp->btgn', xi, c_im)
    y = y.reshape(bsz, t, S5_WIDTH) + d_skip.astype(F32) * u
    z = jax.nn.gelu(y)
    out = z * jax.nn.sigmoid(z @ glu_w.astype(F32) + glu_b.astype(F32))
    return out, xr[:, -1], xi[:, -1]


def peer_block(xb, w_q, sub_keys, u_tab, v_tab):
    n = xb.shape[0]
    q = (xb @ w_q).astype(F32).reshape(n, PEER_HEADS, 2, PEER_HALF)
    s = jnp.einsum('nhcd,hckd->nhck', q, sub_keys.astype(F32))
    top_s, top_i = lax.top_k(s, PEER_TOPK)
    cand = top_s[:, :, 0, :, None] + top_s[:, :, 1, None, :]
    best_s, best_i = lax.top_k(cand.reshape(n, PEER_HEADS, PEER_TOPK * PEER_TOPK), PEER_TOPK)
    e1 = jnp.take_along_axis(top_i[:, :, 0], best_i // PEER_TOPK, axis=-1)
    e2 = jnp.take_along_axis(top_i[:, :, 1], best_i % PEER_TOPK, axis=-1)
    expert = e1 * PEER_NKEYS + e2
    gate = jax.nn.softmax(best_s, -1)
    u = jnp.take(u_tab, expert, axis=0)
    v = jnp.take(v_tab, expert, axis=0)
    hid = jax.nn.gelu(jnp.einsum('nhkd,nd->nhk', u, xb).astype(F32)) * gate
    return jnp.einsum('nhk,nhkd->nd', hid.astype(xb.dtype), v)


def peer(x, w_q, sub_keys, u_tab, v_tab):
    b, t, d = x.shape
    n = b * t
    nb = -(-n // PEER_BLOCK)
    flat = jnp.pad(x.reshape(n, d), ((0, nb * PEER_BLOCK - n), (0, 0)))
    out = lax.map(lambda xb: peer_block(xb, w_q, sub_keys, u_tab, v_tab), flat.reshape(nb, PEER_BLOCK, d))
    return out.reshape(nb * PEER_BLOCK, d)[:n].reshape(b, t, d)


def trunk_layer(x, pos, ret_s, gdn_s, conv_s, s5_re, s5_im,
                norm_mix, w_in, w_out, ret_norm, conv_w, a_log, dt_bias, gdn_norm,
                lam_re, lam_im, log_dt, b_re, b_im, c_re, c_im, d_skip, glu_w, glu_b,
                norm_ffn, peer_wq, peer_keys, peer_u, peer_v):
    h = (rmsnorm(x, norm_mix) @ w_in).astype(F32)
    offs = [int(o) for o in np.cumsum(IN_SIZES)[:-1]]
    r_q, r_k, r_v, r_g, g_qkv, g_g, g_a, g_b, s_u = jnp.split(h, offs, axis=-1)
    o_ret, ret_new = retention_mixer(r_q, r_k, r_v, r_g, pos, ret_norm, ret_s.astype(F32))
    o_gdn, gdn_new, conv_new = gdn_mixer(g_qkv, g_g, g_a, g_b, conv_s, conv_w, a_log, dt_bias, gdn_norm,
                                         gdn_s.astype(F32))
    o_s5, s5_re_new, s5_im_new = s5_mixer(s_u, s5_re, s5_im, lam_re, lam_im, log_dt, b_re, b_im,
                                          c_re, c_im, d_skip, glu_w, glu_b)
    mix = jnp.concatenate([o_ret, o_gdn, o_s5], -1).astype(x.dtype)
    x = x + mix @ w_out
    x = x + peer(rmsnorm(x, norm_ffn), peer_wq, peer_keys, peer_u, peer_v)
    return x, ret_new, gdn_new, conv_new, s5_re_new, s5_im_new


def setup_inputs(seed: int = 0) -> dict:
    key = jax.random.key(seed)
    ks = iter(jax.random.split(key, 40))
    nrm = lambda shape, scale: jax.random.normal(next(ks), shape, F32) * scale
    uni = lambda shape, lo, hi: jax.random.uniform(next(ks), shape, F32, lo, hi)
    n_idx = jnp.arange(S5_STATE, dtype=F32)
    dt_g = jnp.exp(uni((DEPTH, GDN_HEADS), math.log(1e-3), math.log(1e-1)))
    return {
        'x_prompt': nrm((BATCH, SEQ, D_MODEL), 1.0),
        'x_sample': nrm((DEC_BATCH, DEC_SEQ, D_MODEL), 1.0),
        'state_ret': nrm((DEPTH, DEC_BATCH, RET_HEADS, RET_DK, RET_DV), 1.0),
        'state_gdn': nrm((DEPTH, DEC_BATCH, GDN_HEADS, GDN_DK, GDN_DV), 0.1),
        'cache_gdn_conv': nrm((DEPTH, DEC_BATCH, CONV_W - 1, GDN_CONV_CH), 1.0),
        'state_s5_re': nrm((DEPTH, DEC_BATCH, S5_GROUPS, S5_STATE), 0.5),
        'state_s5_im': nrm((DEPTH, DEC_BATCH, S5_GROUPS, S5_STATE), 0.5),
        'norm_mix': 1.0 + nrm((DEPTH, D_MODEL), 0.02),
        'w_in': nrm((DEPTH, D_MODEL, IN_COLS), D_MODEL ** -0.5),
        'w_out': nrm((DEPTH, MIX_WIDTH, D_MODEL), MIX_WIDTH ** -0.5),
        'ret_norm': 1.0 + nrm((DEPTH, RET_WIDTH), 0.02),
        'gdn_conv_w': nrm((DEPTH, CONV_W, GDN_CONV_CH), CONV_W ** -0.5),
        'gdn_a_log': jnp.log(uni((DEPTH, GDN_HEADS), 1.0, 16.0)),
        'gdn_dt_bias': dt_g + jnp.log(-jnp.expm1(-dt_g)),
        'gdn_norm': 1.0 + nrm((DEPTH, GDN_DV), 0.02),
        's5_lam_re': -0.5 + nrm((DEPTH, S5_GROUPS, S5_STATE), 0.01),
        's5_lam_im': math.pi * n_idx + nrm((DEPTH, S5_GROUPS, S5_STATE), 0.01),
        's5_log_dt': uni((DEPTH, S5_GROUPS), math.log(1e-3), math.log(1e-1)),
        's5_b_re': nrm((DEPTH, S5_GROUPS, S5_STATE, S5_GROUP_CH), (2 * S5_GROUP_CH) ** -0.5),
        's5_b_im': nrm((DEPTH, S5_GROUPS, S5_STATE, S5_GROUP_CH), (2 * S5_GROUP_CH) ** -0.5),
        's5_c_re': nrm((DEPTH, S5_GROUPS, S5_GROUP_CH, S5_STATE), S5_STATE ** -0.5),
        's5_c_im': nrm((DEPTH, S5_GROUPS, S5_GROUP_CH, S5_STATE), S5_STATE ** -0.5),
        's5_d': nrm((DEPTH, S5_WIDTH), 0.5),
        's5_glu_w': nrm((DEPTH, S5_WIDTH, S5_WIDTH), S5_WIDTH ** -0.5),
        's5_glu_b': nrm((DEPTH, S5_WIDTH), 0.01),
        'norm_ffn': 1.0 + nrm((DEPTH, D_MODEL), 0.02),
        'peer_wq': nrm((DEPTH, D_MODEL, PEER_HEADS * PEER_DQ), D_MODEL ** -0.5),
        'peer_keys': nrm((DEPTH, PEER_HEADS, 2, PEER_NKEYS, PEER_HALF), PEER_HALF ** -0.5),
        'peer_u': nrm((DEPTH, PEER_EXPERTS, D_MODEL), D_MODEL ** -0.5),
        'peer_v': nrm((DEPTH, PEER_EXPERTS, D_MODEL), PEER_HEADS ** -0.5),
        'norm_final': 1.0 + nrm((D_MODEL,), 0.02),
    }


def reference(x_prompt, x_sample, state_ret, state_gdn, cache_gdn_conv, state_s5_re, state_s5_im,
              norm_mix, w_in, w_out, ret_norm, gdn_conv_w, gdn_a_log, gdn_dt_bias, gdn_norm,
              s5_lam_re, s5_lam_im, s5_log_dt, s5_b_re, s5_b_im, s5_c_re, s5_c_im, s5_d,
              s5_glu_w, s5_glu_b, norm_ffn, peer_wq, peer_keys, peer_u, peer_v, norm_final):
    bp, tp = x_prompt.shape[:2]
    ts = x_sample.shape[1]
    pos_p = jnp.arange(tp)
    pos_s = PAST_LEN + jnp.arange(ts)
    z_ret = jnp.zeros((bp, RET_HEADS, RET_DK, RET_DV), F32)
    z_gdn = jnp.zeros((bp, GDN_HEADS, GDN_DK, GDN_DV), F32)
    z_conv = jnp.zeros((bp, CONV_W - 1, GDN_CONV_CH), F32)
    z_s5 = jnp.zeros((bp, S5_GROUPS, S5_STATE), F32)
    xp, xs = x_prompt, x_sample
    pr, pg, pc, pre, pim = [], [], [], [], []
    sr, sg, sc, sre, sim = [], [], [], [], []
    for l in range(DEPTH):
        w = (norm_mix[l], w_in[l], w_out[l], ret_norm[l], gdn_conv_w[l], gdn_a_log[l], gdn_dt_bias[l],
             gdn_norm[l], s5_lam_re[l], s5_lam_im[l], s5_log_dt[l], s5_b_re[l], s5_b_im[l], s5_c_re[l],
             s5_c_im[l], s5_d[l], s5_glu_w[l], s5_glu_b[l], norm_ffn[l], peer_wq[l], peer_keys[l],
             peer_u[l], peer_v[l])
        xp, a1, a2, a3, a4, a5 = trunk_layer(xp, pos_p, z_ret, z_gdn, z_conv, z_s5, z_s5, *w)
        xs, b1, b2, b3, b4, b5 = trunk_layer(xs, pos_s, state_ret[l], state_gdn[l], cache_gdn_conv[l],
                                             state_s5_re[l], state_s5_im[l], *w)
        pr.append(a1); pg.append(a2); pc.append(a3); pre.append(a4); pim.append(a5)
        sr.append(b1); sg.append(b2); sc.append(b3); sre.append(b4); sim.append(b5)
    y_prompt = rmsnorm(xp, norm_final)
    y_sample = rmsnorm(xs, norm_final)
    p_ret, p_gdn, p_conv, p_s5_re, p_s5_im = (jnp.stack(pr), jnp.stack(pg), jnp.stack(pc), jnp.stack(pre), jnp.stack(pim))
    s_ret, s_gdn, s_conv, s_s5_re, s_s5_im = (jnp.stack(sr), jnp.stack(sg), jnp.stack(sc), jnp.stack(sre), jnp.stack(sim))
    return (y_prompt, y_sample, p_ret, p_gdn, p_conv, p_s5_re, p_s5_im, s_ret, s_gdn, s_conv, s_s5_re, s_s5_im)
```

```python
import functools
import math

import numpy as np
import jax
import jax.numpy as jnp
from jax import lax
from jax.experimental import pallas as pl
from jax.experimental.pallas import tpu as pltpu

F32 = jnp.float32
BF16 = jnp.bfloat16
MXU_DTYPE = BF16
HIGHEST = lax.Precision.HIGHEST

EPS = 1e-6
CHUNK = 64
D_MODEL = 4096
RET_HEADS, RET_DK, RET_DV = 8, 64, 128
RET_QK = RET_HEADS * RET_DK
RET_WIDTH = RET_HEADS * RET_DV
ROPE_BASE = 10000.0
GDN_HEADS, GDN_DK, GDN_DV = 8, 128, 128
GDN_WIDTH = GDN_HEADS * GDN_DV
CONV_W = 4
GDN_CONV_CH = GDN_HEADS * (2 * GDN_DK + GDN_DV)
S5_GROUP_CH, S5_GROUPS, S5_STATE = 16, 128, 64
S5_WIDTH = S5_GROUP_CH * S5_GROUPS
S5_LANES = S5_GROUPS * S5_STATE
PEER_HEADS, PEER_NKEYS, PEER_TOPK = 8, 128, 16
PEER_EXPERTS = PEER_NKEYS * PEER_NKEYS
PEER_HALF = 128
PEER_SLOTS = PEER_HEADS * PEER_TOPK

C_RQ, C_RK, C_RV, C_RG, C_GQKV, C_GG, C_SU = 0, 512, 1024, 2048, 3072, 6144, 7168
H_COLS = 9216
IN_A0 = 2 * RET_QK + 2 * RET_WIDTH + GDN_CONV_CH + GDN_WIDTH
AB_PAD = 128

S5_RUN = 32
S5_ROWS = 8 * S5_RUN
S5_LB = 512
S5_CB = S5_LB // S5_STATE * S5_GROUP_CH
S5_NCB = S5_LANES // S5_LB

VMEM_LIMIT = 56 * 1024 * 1024


def _cparams(sem):
    return pltpu.CompilerParams(dimension_semantics=sem, vmem_limit_bytes=VMEM_LIMIT)


def _dot(a, b, precision=None):
    return jnp.dot(a, b, preferred_element_type=F32, precision=precision)


def _dot_nt(a, b, precision=None):
    return lax.dot_general(a, b, (((1,), (1,)), ((), ())), preferred_element_type=F32,
                           precision=precision)


def _mm(x):
    return x.astype(MXU_DTYPE)


def _sigmoid(x):
    return 1.0 / (1.0 + jnp.exp(-x))


def _silu(x):
    return x * _sigmoid(x)


def _gelu_tanh(x):
    return 0.5 * x * (1.0 + jnp.tanh(math.sqrt(2.0 / math.pi) * (x + 0.044715 * (x * x * x))))


def _norm_mm_kernel(x_ref, g_ref, w_ref, we_ref, o_ref, xn_ref, oe_ref):
    @pl.when(pl.program_id(1) == 0)
    def _():
        x = x_ref[...]
        ms = jnp.mean(x * x, axis=-1, keepdims=True)
        xn = _mm(x * lax.rsqrt(ms + EPS) * g_ref[...])
        xn_ref[...] = xn
        oe_ref[...] = _dot(xn, we_ref[...])

    o_ref[...] = _dot(xn_ref[...], w_ref[...])


def _norm_mm(x, g, w, w_extra, tm=512, tn=512):
    n, d = x.shape
    nc = w.shape[1]
    ne = w_extra.shape[1]
    return pl.pallas_call(
        _norm_mm_kernel,
        grid=(n // tm, nc // tn),
        in_specs=[pl.BlockSpec((tm, d), lambda i, j: (i, 0)),
                  pl.BlockSpec((1, d), lambda i, j: (0, 0)),
                  pl.BlockSpec((d, tn), lambda i, j: (0, j)),
                  pl.BlockSpec((d, ne), lambda i, j: (0, 0))],
        out_specs=[pl.BlockSpec((tm, tn), lambda i, j: (i, j)),
                   pl.BlockSpec((tm, d), lambda i, j: (i, 0)),
                   pl.BlockSpec((tm, ne), lambda i, j: (i, 0))],
        out_shape=[jax.ShapeDtypeStruct((n, nc), F32),
                   jax.ShapeDtypeStruct((n, d), MXU_DTYPE),
                   jax.ShapeDtypeStruct((n, ne), F32)],
        compiler_params=_cparams(("parallel", "arbitrary")),
        name="norm_proj",
    )(x, g.reshape(1, d), w, w_extra)


def _stream_of_chunk(c, n_prompt_chunks):
    return jnp.where(c < n_prompt_chunks, 0, c - n_prompt_chunks + 1)


def _ret_kernel(n_prompt_chunks, q_ref, k_ref, v_ref, gate_ref, cos_ref, sin_ref, s0_ref, ng_ref,
                o_ref, sout_ref, s_scr):
    c = pl.program_id(0)

    @pl.when(jnp.logical_or(c == 0, c >= n_prompt_chunks))
    def _():
        s_scr[...] = s0_ref[0]

    cos = cos_ref[...]
    sin = sin_ref[...]
    lane = lax.broadcasted_iota(jnp.int32, (CHUNK, RET_QK), 1)
    first_half = (lane % RET_DK) < (RET_DK // 2)

    def rope(x):
        swapped = jnp.where(first_half, pltpu.roll(x, RET_QK - RET_DK // 2, 1),
                            pltpu.roll(x, RET_DK // 2, 1))
        return x * cos + swapped * sin

    q = rope(q_ref[...])
    k = rope(k_ref[...]) * (RET_DK ** -0.5)
    v = v_ref[...]
    gate = gate_ref[...]
    ng = ng_ref[...]

    ti = lax.broadcasted_iota(jnp.int32, (CHUNK, CHUNK), 0)
    tj = lax.broadcasted_iota(jnp.int32, (CHUNK, CHUNK), 1)
    diff = (ti - tj).astype(F32)
    pos = lax.broadcasted_iota(jnp.int32, (CHUNK, 1), 0).astype(F32)

    for h in range(RET_HEADS):
        ld = math.log(1.0 - 2.0 ** (-5.0 - h))
        dmat = jnp.where(diff >= 0, jnp.exp(ld * jnp.maximum(diff, 0.0)), 0.0)
        qh = q[:, h * RET_DK:(h + 1) * RET_DK]
        kh = k[:, h * RET_DK:(h + 1) * RET_DK]
        vh = v[:, h * RET_DV:(h + 1) * RET_DV]
        s = s_scr[h]
        scores = _dot_nt(_mm(qh), _mm(kh)) * dmat
        o = _dot(_mm(scores), _mm(vh))
        o = o + _dot(_mm(qh), _mm(s)) * jnp.exp(ld * (pos + 1.0))
        kd = kh * jnp.exp(ld * (CHUNK - 1.0 - pos))
        s_scr[h] = s * math.exp(ld * CHUNK) + _dot(_mm(kd.T), _mm(vh))
        mu = jnp.mean(o, axis=-1, keepdims=True)
        oc = o - mu
        var = jnp.mean(oc * oc, axis=-1, keepdims=True)
        o = oc * lax.rsqrt(var + EPS) * ng[:, h * RET_DV:(h + 1) * RET_DV]
        o = o * _silu(gate[:, h * RET_DV:(h + 1) * RET_DV])
        o_ref[:, h * RET_DV:(h + 1) * RET_DV] = o.astype(o_ref.dtype)

    sout_ref[0] = s_scr[...]


def _retention(h, cos, sin, s0, ret_norm, n_prompt_chunks):
    n = h.shape[0]
    nchunks = n // CHUNK
    nstreams = s0.shape[0]
    smap = lambda c: (_stream_of_chunk(c, n_prompt_chunks), 0, 0, 0)
    return pl.pallas_call(
        functools.partial(_ret_kernel, n_prompt_chunks),
        grid=(nchunks,),
        in_specs=[pl.BlockSpec((CHUNK, RET_QK), lambda c: (c, C_RQ // RET_QK)),
                  pl.BlockSpec((CHUNK, RET_QK), lambda c: (c, C_RK // RET_QK)),
                  pl.BlockSpec((CHUNK, RET_WIDTH), lambda c: (c, C_RV // RET_WIDTH)),
                  pl.BlockSpec((CHUNK, RET_WIDTH), lambda c: (c, C_RG // RET_WIDTH)),
                  pl.BlockSpec((CHUNK, RET_QK), lambda c: (c, 0)),
                  pl.BlockSpec((CHUNK, RET_QK), lambda c: (c, 0)),
                  pl.BlockSpec((1, RET_HEADS, RET_DK, RET_DV), smap),
                  pl.BlockSpec((1, RET_WIDTH), lambda c: (0, 0))],
        out_specs=[pl.BlockSpec((CHUNK, RET_WIDTH), lambda c: (c, 0)),
                   pl.BlockSpec((1, RET_HEADS, RET_DK, RET_DV), smap)],
        out_shape=[jax.ShapeDtypeStruct((n, RET_WIDTH), MXU_DTYPE),
                   jax.ShapeDtypeStruct((nstreams, RET_HEADS, RET_DK, RET_DV), F32)],
        scratch_shapes=[pltpu.VMEM((RET_HEADS, RET_DK, RET_DV), F32)],
        compiler_params=_cparams(("arbitrary",)),
        name="retention",
    )(h, h, h, h, cos, sin, s0, ret_norm.reshape(1, RET_WIDTH))


HIST_ROW0 = 8 - (CONV_W - 1)


def _unit_lower_inverse(a):
    n = a.shape[0]
    ti = lax.broadcasted_iota(jnp.int32, (n, n), 0)
    tj = lax.broadcasted_iota(jnp.int32, (n, n), 1)
    eye = (ti == tj).astype(F32)
    inv = eye - a
    p = a
    k = 2
    while k < n:
        p = _dot(p, p, HIGHEST)
        inv = inv + _dot(inv, p, HIGHEST)
        k *= 2
    return inv


def _gdn_kernel(n_prompt_chunks, qkv_ref, gate_ref, ab_ref, cw_ref, alog_ref, dtb_ref, ng_ref,
                hist0_ref, s0_ref, o_ref, hist_out_ref, sout_ref, xin_scr, s_scr):
    c = pl.program_id(0)

    @pl.when(jnp.logical_or(c == 0, c >= n_prompt_chunks))
    def _():
        s_scr[...] = s0_ref[0]
        xin_scr[HIST_ROW0:8, :] = hist0_ref[0]

    xin_scr[8:8 + CHUNK, :] = qkv_ref[...]
    cw = cw_ref[...]
    y = xin_scr[HIST_ROW0:HIST_ROW0 + CHUNK, :] * cw[0:1, :]
    for w in range(1, CONV_W):
        y = y + xin_scr[HIST_ROW0 + w:HIST_ROW0 + w + CHUNK, :] * cw[w:w + 1, :]
    y = _silu(y)
    new_hist = xin_scr[8 + CHUNK - (CONV_W - 1):8 + CHUNK, :]
    hist_out_ref[0] = new_hist
    xin_scr[HIST_ROW0:8, :] = new_hist

    ab = ab_ref[...]
    x = ab + dtb_ref[...]
    softplus = jnp.maximum(x, 0.0) + jnp.log1p(jnp.exp(-jnp.abs(x)))
    g_all = -jnp.exp(alog_ref[...]) * softplus
    beta_all = _sigmoid(ab)

    ti = lax.broadcasted_iota(jnp.int32, (CHUNK, CHUNK), 0)
    tj = lax.broadcasted_iota(jnp.int32, (CHUNK, CHUNK), 1)
    lower = ti >= tj
    strict = ti > tj
    gc_all = _dot(lower.astype(F32), g_all, HIGHEST)
    gc_t = gc_all.T
    gate = gate_ref[...]
    ng = ng_ref[...]

    for h in range(GDN_HEADS):
        gcol = gc_all[:, h:h + 1]
        grow = gc_t[h:h + 1, :]
        decay = jnp.where(lower, jnp.exp(jnp.where(lower, gcol - grow, 0.0)), 0.0)
        beta = beta_all[:, GDN_HEADS + h:GDN_HEADS + h + 1]
        qc = y[:, h * GDN_DK:(h + 1) * GDN_DK]
        kc = y[:, GDN_WIDTH + h * GDN_DK:GDN_WIDTH + (h + 1) * GDN_DK]
        vc = y[:, 2 * GDN_WIDTH + h * GDN_DV:2 * GDN_WIDTH + (h + 1) * GDN_DV]
        qh = qc * lax.rsqrt(jnp.sum(qc * qc, axis=-1, keepdims=True) + EPS) * (GDN_DK ** -0.5)
        kh = kc * lax.rsqrt(jnp.sum(kc * kc, axis=-1, keepdims=True) + EPS)
        kb = kh * beta
        a_mat = jnp.where(strict, _dot_nt(_mm(kb), _mm(kh)) * decay, 0.0)
        t_inv = _unit_lower_inverse(a_mat)
        egc = jnp.exp(gcol)
        rhs = jnp.concatenate([vc * beta, kb * egc], axis=-1)
        sol = _dot(t_inv, rhs, HIGHEST)
        u = sol[:, :GDN_DV]
        w = sol[:, GDN_DV:]
        s = s_scr[h]
        sm = _mm(s)
        v_new = u - _dot(_mm(w), sm)
        attn = _dot_nt(_mm(qh), _mm(kh)) * decay
        o = _dot(_mm(qh * egc), sm) + _dot(_mm(attn), _mm(v_new))
        g_last = gc_all[CHUNK - 1:CHUNK, h:h + 1]
        kd = kh * jnp.exp(g_last - gcol)
        s_scr[h] = s * jnp.exp(g_last) + _dot(_mm(kd.T), _mm(v_new))
        o = o * lax.rsqrt(jnp.mean(o * o, axis=-1, keepdims=True) + EPS) * ng
        o = o * _silu(gate[:, h * GDN_DV:(h + 1) * GDN_DV])
        o_ref[:, h * GDN_DV:(h + 1) * GDN_DV] = o.astype(o_ref.dtype)

    sout_ref[0] = s_scr[...]


def _gdn(h, ab, conv_w, a_log, dt_bias, gdn_norm, hist0, s0, n_prompt_chunks):
    n = h.shape[0]
    nchunks = n // CHUNK
    nstreams = s0.shape[0]
    smap4 = lambda c: (_stream_of_chunk(c, n_prompt_chunks), 0, 0, 0)
    smap3 = lambda c: (_stream_of_chunk(c, n_prompt_chunks), 0, 0)
    alog_row = jnp.zeros((1, AB_PAD), F32).at[0, :GDN_HEADS].set(a_log)
    dtb_row = jnp.zeros((1, AB_PAD), F32).at[0, :GDN_HEADS].set(dt_bias)
    return pl.pallas_call(
        functools.partial(_gdn_kernel, n_prompt_chunks),
        grid=(nchunks,),
        in_specs=[pl.BlockSpec((CHUNK, GDN_CONV_CH), lambda c: (c, C_GQKV // GDN_CONV_CH)),
                  pl.BlockSpec((CHUNK, GDN_WIDTH), lambda c: (c, C_GG // GDN_WIDTH)),
                  pl.BlockSpec((CHUNK, AB_PAD), lambda c: (c, 0)),
                  pl.BlockSpec((CONV_W, GDN_CONV_CH), lambda c: (0, 0)),
                  pl.BlockSpec((1, AB_PAD), lambda c: (0, 0)),
                  pl.BlockSpec((1, AB_PAD), lambda c: (0, 0)),
                  pl.BlockSpec((1, GDN_DV), lambda c: (0, 0)),
                  pl.BlockSpec((1, CONV_W - 1, GDN_CONV_CH), smap3),
                  pl.BlockSpec((1, GDN_HEADS, GDN_DK, GDN_DV), smap4)],
        out_specs=[pl.BlockSpec((CHUNK, GDN_WIDTH), lambda c: (c, 0)),
                   pl.BlockSpec((1, CONV_W - 1, GDN_CONV_CH), smap3),
                   pl.BlockSpec((1, GDN_HEADS, GDN_DK, GDN_DV), smap4)],
        out_shape=[jax.ShapeDtypeStruct((n, GDN_WIDTH), MXU_DTYPE),
                   jax.ShapeDtypeStruct((nstreams, CONV_W - 1, GDN_CONV_CH), F32),
                   jax.ShapeDtypeStruct((nstreams, GDN_HEADS, GDN_DK, GDN_DV), F32)],
        scratch_shapes=[pltpu.VMEM((8 + CHUNK, GDN_CONV_CH), F32),
                        pltpu.VMEM((GDN_HEADS, GDN_DK, GDN_DV), F32)],
        compiler_params=_cparams(("arbitrary",)),
        name="gated_delta",
    )(h, h, ab, conv_w, alog_row, dtb_row, gdn_norm.reshape(1, GDN_DV), hist0, s0)


def _s5_prep_kernel(lre_ref, lim_ref, ldt_ref, btr_ref, bti_ref,
                    bbr_ref, bbi_ref, apr_ref, api_ref):
    lam_re = lre_ref[...]
    lam_im = lim_ref[...]
    dt = jnp.exp(ldt_ref[...])
    mag = jnp.exp(lam_re * dt)
    a_re = mag * jnp.cos(lam_im * dt)
    a_im = mag * jnp.sin(lam_im * dt)
    den = lam_re * lam_re + lam_im * lam_im
    f_re = ((a_re - 1.0) * lam_re + a_im * lam_im) / den
    f_im = (a_im * lam_re - (a_re - 1.0) * lam_im) / den
    btr = btr_ref[...]
    bti = bti_ref[...]
    bbr_ref[...] = f_re * btr - f_im * bti
    bbi_ref[...] = f_re * bti + f_im * btr
    p_re, p_im = a_re, a_im
    apr_ref[0] = p_re
    api_ref[0] = p_im
    for v in range(1, S5_RUN):
        p_re, p_im = p_re * a_re - p_im * a_im, p_re * a_im + p_im * a_re
        apr_ref[v] = p_re
        api_ref[v] = p_im


def _s5_prep(lam_re, lam_im, log_dt, b_re, b_im):
    g, p = lam_re.shape
    nch = b_re.shape[-1]
    return pl.pallas_call(
        _s5_prep_kernel,
        out_shape=[jax.ShapeDtypeStruct((g, nch, p), F32), jax.ShapeDtypeStruct((g, nch, p), F32),
                   jax.ShapeDtypeStruct((S5_RUN, g, 1, p), F32),
                   jax.ShapeDtypeStruct((S5_RUN, g, 1, p), F32)],
        name="s5_discretise",
    )(lam_re.reshape(g, 1, p), lam_im.reshape(g, 1, p), log_dt.reshape(g, 1, 1),
      jnp.swapaxes(b_re, 1, 2), jnp.swapaxes(b_im, 1, 2))


def _s5_kernel(n_prompt_blocks, runs_per_stream, u_ref, bbw_ref, cre_ref, cim_ref, apr_ref, api_ref,
               x0r_ref, x0i_ref, d_ref, z_ref, endr_ref, endi_ref, xr_scr, xi_scr, car_scr, cai_scr):
    b = pl.program_id(1)
    u_perm = jnp.concatenate([u_ref[pl.ds(v, 8, stride=S5_RUN), :] for v in range(S5_RUN)], axis=0)
    e = _dot(_mm(u_perm), bbw_ref[0])
    a_re = jnp.broadcast_to(apr_ref[0:1, :], (8, S5_LB))
    a_im = jnp.broadcast_to(api_ref[0:1, :], (8, S5_LB))

    xr = e[0:8, :S5_LB]
    xi = e[0:8, S5_LB:]
    xr_scr[0:8, :] = xr
    xi_scr[0:8, :] = xi
    for v in range(1, S5_RUN):
        er = e[8 * v:8 * v + 8, :S5_LB]
        ei = e[8 * v:8 * v + 8, S5_LB:]
        xr, xi = a_re * xr - a_im * xi + er, a_re * xi + a_im * xr + ei
        xr_scr[8 * v:8 * v + 8, :] = xr
        xi_scr[8 * v:8 * v + 8, :] = xi

    al_re = apr_ref[S5_RUN - 1:S5_RUN, :]
    al_im = api_ref[S5_RUN - 1:S5_RUN, :]
    x0r = x0r_ref[0]
    x0i = x0i_ref[0]
    is_sample = b >= n_prompt_blocks
    prev_r = car_scr[...]
    prev_i = cai_scr[...]
    cin_r, cin_i, end_r, end_i = [], [], [], []
    for r in range(8):
        if r % runs_per_stream == 0:
            reset = jnp.logical_or(is_sample, b == 0) if r == 0 else is_sample
        else:
            reset = None
        if reset is None:
            cr, ci = prev_r, prev_i
        else:
            cr = jnp.where(reset, x0r[r:r + 1, :], prev_r)
            ci = jnp.where(reset, x0i[r:r + 1, :], prev_i)
        prev_r = al_re * cr - al_im * ci + xr[r:r + 1, :]
        prev_i = al_re * ci + al_im * cr + xi[r:r + 1, :]
        cin_r.append(cr)
        cin_i.append(ci)
        end_r.append(prev_r)
        end_i.append(prev_i)
    car_scr[...] = prev_r
    cai_scr[...] = prev_i
    endr_ref[0] = jnp.concatenate(end_r, axis=0)
    endi_ref[0] = jnp.concatenate(end_i, axis=0)
    cr = jnp.concatenate(cin_r, axis=0)
    ci = jnp.concatenate(cin_i, axis=0)

    for v in range(S5_RUN):
        pr = jnp.broadcast_to(apr_ref[v:v + 1, :], (8, S5_LB))
        pi = jnp.broadcast_to(api_ref[v:v + 1, :], (8, S5_LB))
        xr_scr[8 * v:8 * v + 8, :] = xr_scr[8 * v:8 * v + 8, :] + (pr * cr - pi * ci)
        xi_scr[8 * v:8 * v + 8, :] = xi_scr[8 * v:8 * v + 8, :] + (pr * ci + pi * cr)

    y = _dot(_mm(xr_scr[...]), cre_ref[0]) - _dot(_mm(xi_scr[...]), cim_ref[0])
    z = _gelu_tanh(y + d_ref[...] * u_perm)
    for v in range(S5_RUN):
        z_ref[pl.ds(v, 8, stride=S5_RUN), :] = z[8 * v:8 * v + 8, :]


def _s5_scan(h, bbw, cre, cim, apow_re, apow_im, x0_re, x0_im, d_skip, n_prompt_blocks, runs_per_stream):
    n = h.shape[0]
    nblocks = n // S5_ROWS
    xmap = lambda j, b: (jnp.where(b < n_prompt_blocks, 0, b - n_prompt_blocks + 1), 0, j)
    return pl.pallas_call(
        functools.partial(_s5_kernel, n_prompt_blocks, runs_per_stream),
        grid=(S5_NCB, nblocks),
        in_specs=[pl.BlockSpec((S5_ROWS, S5_CB), lambda j, b: (b, C_SU // S5_CB + j)),
                  pl.BlockSpec((1, S5_CB, 2 * S5_LB), lambda j, b: (j, 0, 0)),
                  pl.BlockSpec((1, S5_LB, S5_CB), lambda j, b: (j, 0, 0)),
                  pl.BlockSpec((1, S5_LB, S5_CB), lambda j, b: (j, 0, 0)),
                  pl.BlockSpec((S5_RUN, S5_LB), lambda j, b: (0, j)),
                  pl.BlockSpec((S5_RUN, S5_LB), lambda j, b: (0, j)),
                  pl.BlockSpec((1, 8, S5_LB), xmap),
                  pl.BlockSpec((1, 8, S5_LB), xmap),
                  pl.BlockSpec((1, S5_CB), lambda j, b: (0, j))],
        out_specs=[pl.BlockSpec((S5_ROWS, S5_CB), lambda j, b: (b, j)),
                   pl.BlockSpec((1, 8, S5_LB), lambda j, b: (b, 0, j)),
                   pl.BlockSpec((1, 8, S5_LB), lambda j, b: (b, 0, j))],
        out_shape=[jax.ShapeDtypeStruct((n, S5_WIDTH), F32),
                   jax.ShapeDtypeStruct((nblocks, 8, S5_LANES), F32),
                   jax.ShapeDtypeStruct((nblocks, 8, S5_LANES), F32)],
        scratch_shapes=[pltpu.VMEM((S5_ROWS, S5_LB), F32), pltpu.VMEM((S5_ROWS, S5_LB), F32),
                        pltpu.VMEM((1, S5_LB), F32), pltpu.VMEM((1, S5_LB), F32)],
        compiler_params=_cparams(("parallel", "arbitrary")),
        name="s5_scan",
    )(h, bbw, cre, cim, apow_re, apow_im, x0_re, x0_im, d_skip.reshape(1, S5_WIDTH))


def _glu_kernel(zr_ref, w_ref, b_ref, zc_ref, o_ref):
    lin = _dot(_mm(zr_ref[...]), w_ref[...]) + b_ref[...]
    o_ref[...] = (zc_ref[...] * _sigmoid(lin)).astype(o_ref.dtype)


def _glu(z, w, bias, tm=512, tn=512):
    n, k = z.shape
    nc = w.shape[1]
    return pl.pallas_call(
        _glu_kernel,
        grid=(n // tm, nc // tn),
        in_specs=[pl.BlockSpec((tm, k), lambda i, j: (i, 0)),
                  pl.BlockSpec((k, tn), lambda i, j: (0, j)),
                  pl.BlockSpec((1, tn), lambda i, j: (0, j)),
                  pl.BlockSpec((tm, tn), lambda i, j: (i, j))],
        out_specs=pl.BlockSpec((tm, tn), lambda i, j: (i, j)),
        out_shape=jax.ShapeDtypeStruct((n, nc), MXU_DTYPE),
        compiler_params=_cparams(("parallel", "parallel")),
        name="s5_glu",
    )(z, w, bias.reshape(1, nc), z)


def _mix_out_kernel(x_ref, a_ref, b_ref, c_ref, w_ref, o_ref):
    ka = a_ref.shape[1]
    kb = b_ref.shape[1]
    acc = _dot(a_ref[...], w_ref[0:ka, :])
    acc = acc + _dot(b_ref[...], w_ref[ka:ka + kb, :])
    acc = acc + _dot(c_ref[...], w_ref[ka + kb:, :])
    o_ref[...] = x_ref[...] + acc


def _mix_out(x, o_ret, o_gdn, o_s5, w, tm=512, tn=512):
    n, d = x.shape
    k = w.shape[0]
    return pl.pallas_call(
        _mix_out_kernel,
        grid=(n // tm, d // tn),
        in_specs=[pl.BlockSpec((tm, tn), lambda i, j: (i, j)),
                  pl.BlockSpec((tm, o_ret.shape[1]), lambda i, j: (i, 0)),
                  pl.BlockSpec((tm, o_gdn.shape[1]), lambda i, j: (i, 0)),
                  pl.BlockSpec((tm, o_s5.shape[1]), lambda i, j: (i, 0)),
                  pl.BlockSpec((k, tn), lambda i, j: (0, j))],
        out_specs=pl.BlockSpec((tm, tn), lambda i, j: (i, j)),
        out_shape=jax.ShapeDtypeStruct((n, d), F32),
        compiler_params=_cparams(("parallel", "parallel")),
        name="mix_out_proj",
    )(x, o_ret, o_gdn, o_s5, w)


def _mm_kernel(a_ref, w_ref, o_ref):
    o_ref[...] = _dot(a_ref[...], w_ref[...])


def _matmul(a, w, tm=512, tn=1024):
    n, k = a.shape
    nc = w.shape[1]
    return pl.pallas_call(
        _mm_kernel,
        grid=(n // tm, nc // tn),
        in_specs=[pl.BlockSpec((tm, k), lambda i, j: (i, 0)),
                  pl.BlockSpec((k, tn), lambda i, j: (0, j))],
        out_specs=pl.BlockSpec((tm, tn), lambda i, j: (i, j)),
        out_shape=jax.ShapeDtypeStruct((n, nc), F32),
        compiler_params=_cparams(("parallel", "parallel")),
        name="peer_hidden_dense",
    )(a, w)


def _res_mm_kernel(x_ref, a_ref, w_ref, o_ref):
    @pl.when(pl.program_id(2) == 0)
    def _():
        o_ref[...] = x_ref[...]

    o_ref[...] += _dot(a_ref[...], w_ref[...])


def _res_matmul(x, a, w, tm=512, tn=1024, tk=4096):
    n, d = x.shape
    k = a.shape[1]
    return pl.pallas_call(
        _res_mm_kernel,
        grid=(n // tm, d // tn, k // tk),
        in_specs=[pl.BlockSpec((tm, tn), lambda i, j, kk: (i, j)),
                  pl.BlockSpec((tm, tk), lambda i, j, kk: (i, kk)),
                  pl.BlockSpec((tk, tn), lambda i, j, kk: (kk, j))],
        out_specs=pl.BlockSpec((tm, tn), lambda i, j, kk: (i, j)),
        out_shape=jax.ShapeDtypeStruct((n, d), F32),
        compiler_params=_cparams(("parallel", "parallel", "arbitrary")),
        name="peer_out_dense",
    )(x, a, w)


def _final_norm_kernel(x_ref, g_ref, o_ref):
    x = x_ref[...]
    ms = jnp.mean(x * x, axis=-1, keepdims=True)
    o_ref[...] = x * lax.rsqrt(ms + EPS) * g_ref[...]


def _final_norm(x, g, tm=512):
    n, d = x.shape
    return pl.pallas_call(
        _final_norm_kernel,
        grid=(n // tm,),
        in_specs=[pl.BlockSpec((tm, d), lambda i: (i, 0)), pl.BlockSpec((1, d), lambda i: (0, 0))],
        out_specs=pl.BlockSpec((tm, d), lambda i: (i, 0)),
        out_shape=jax.ShapeDtypeStruct((n, d), F32),
        compiler_params=_cparams(("parallel",)),
        name="final_norm",
    )(x, g.reshape(1, d))


NEG_INF = float("-inf")


def _top16_rows(s, ids, val_ref, id_ref):
    big = float(2 ** 20)
    for r in range(PEER_TOPK):
        m = jnp.max(s, axis=0, keepdims=True)
        i = jnp.min(jnp.where(s == m, ids, big), axis=0, keepdims=True)
        s = jnp.where(ids == i, NEG_INF, s)
        val_ref[r:r + 1, :] = m
        id_ref[r:r + 1, :] = i
    return val_ref[...], id_ref[...]


def _peer_topk_kernel(q_ref, keys_ref, e1_ref, e2_ref, gate_ref,
                      v0_scr, i0_scr, v1_scr, i1_scr, vb_scr, ib_scr, e1_scr, e2_scr, g_scr):
    tt = q_ref.shape[0]
    key_ids = lax.broadcasted_iota(jnp.int32, (PEER_NKEYS, tt), 0).astype(F32)
    row16 = lax.broadcasted_iota(jnp.int32, (PEER_TOPK, tt), 0).astype(F32)
    row8 = lax.broadcasted_iota(jnp.int32, (8, tt), 0).astype(F32)
    cand_ids = [row16]
    cand_ok = [row16 >= 0.0]
    for i in range(1, 8):
        cand_ids.append(row8 + 16.0 * i)
        cand_ok.append(row8 < float(PEER_TOPK // (i + 1)))
    cand_ids.append(16.0 * (row8 + 8.0))
    cand_ok.append(row8 >= 0.0)
    cand_ids = jnp.concatenate(cand_ids, axis=0)
    cand_ok = jnp.concatenate(cand_ok, axis=0)

    for h in range(PEER_HEADS):
        col = 2 * h * PEER_HALF
        s0 = _dot_nt(keys_ref[2 * h], _mm(q_ref[:, col:col + PEER_HALF]))
        t0, ti0 = _top16_rows(s0, key_ids, v0_scr, i0_scr)
        s1 = _dot_nt(keys_ref[2 * h + 1], _mm(q_ref[:, col + PEER_HALF:col + 2 * PEER_HALF]))
        t1, ti1 = _top16_rows(s1, key_ids, v1_scr, i1_scr)
        cand = [t0[0:1] + t1]
        for i in range(1, 8):
            cand.append(t0[i:i + 1] + t1[0:8])
        cand.append(t0[8:16] + t1[0:1])
        cand = jnp.where(cand_ok, jnp.concatenate(cand, axis=0), NEG_INF)
        best, flat = _top16_rows(cand, cand_ids, vb_scr, ib_scr)
        flat = flat.astype(jnp.int32)
        i_sel = flat >> 4
        j_sel = flat & (PEER_TOPK - 1)
        e1 = jnp.zeros_like(best)
        e2 = jnp.zeros_like(best)
        for r in range(PEER_TOPK):
            e1 = jnp.where(i_sel == r, ti0[r:r + 1], e1)
            e2 = jnp.where(j_sel == r, ti1[r:r + 1], e2)
        p = jnp.exp(best - best[0:1])
        gate = p / jnp.sum(p, axis=0, keepdims=True)
        e1_scr[h * PEER_TOPK:(h + 1) * PEER_TOPK, :] = e1
        e2_scr[h * PEER_TOPK:(h + 1) * PEER_TOPK, :] = e2
        g_scr[h * PEER_TOPK:(h + 1) * PEER_TOPK, :] = gate
    e1_ref[...] = e1_scr[...].T
    e2_ref[...] = e2_scr[...].T
    gate_ref[...] = g_scr[...].T


def _peer_topk(q, keys, tt=256):
    n = q.shape[0]
    spec = pl.BlockSpec((tt, PEER_SLOTS), lambda i: (i, 0))
    rows16 = pltpu.VMEM((PEER_TOPK, tt), F32)
    slots = pltpu.VMEM((PEER_SLOTS, tt), F32)
    return pl.pallas_call(
        _peer_topk_kernel,
        grid=(n // tt,),
        in_specs=[pl.BlockSpec((tt, q.shape[1]), lambda i: (i, 0)),
                  pl.BlockSpec(keys.shape, lambda i: (0, 0, 0))],
        out_specs=[spec, spec, spec],
        out_shape=[jax.ShapeDtypeStruct((n, PEER_SLOTS), F32)] * 3,
        scratch_shapes=[rows16] * 6 + [slots] * 3,
        compiler_params=_cparams(("parallel",)),
        name="peer_topk",
    )(q, keys)


def _peer_select_kernel(h3_ref, e1_ref, e2_ref, gate_ref, c3_ref):
    tt = h3_ref.shape[0]
    sub = lax.broadcasted_iota(jnp.int32, (PEER_NKEYS, PEER_SLOTS), 0).astype(F32)

    def body(t, carry):
        on1 = sub == e1_ref[pl.ds(t, 1), :]
        on2 = sub == e2_ref[pl.ds(t, 1), :]
        b2 = jnp.where(on2, 1.0, 0.0).astype(MXU_DTYPE)
        ht = h3_ref[t]
        hi = _mm(ht)
        lo = _mm(ht - hi.astype(F32))
        m = _dot(hi, b2) + _dot(lo, b2)
        hid = jnp.sum(jnp.where(on1, m, 0.0), axis=0, keepdims=True)
        coef = _gelu_tanh(hid) * gate_ref[pl.ds(t, 1), :]
        a1 = jnp.where(on1, coef, 0.0).astype(MXU_DTYPE)
        c3_ref[t] = _dot_nt(a1, b2).astype(c3_ref.dtype)
        return carry

    lax.fori_loop(0, tt, body, 0, unroll=4)


def _peer_select(h3, e1, e2, gate, tt=64):
    n = h3.shape[0]
    spec = pl.BlockSpec((tt, PEER_SLOTS), lambda i: (i, 0))
    return pl.pallas_call(
        _peer_select_kernel,
        grid=(n // tt,),
        in_specs=[pl.BlockSpec((tt, PEER_NKEYS, PEER_NKEYS), lambda i: (i, 0, 0)), spec, spec, spec],
        out_specs=pl.BlockSpec((tt, PEER_NKEYS, PEER_NKEYS), lambda i: (i, 0, 0)),
        out_shape=jax.ShapeDtypeStruct((n, PEER_NKEYS, PEER_NKEYS), MXU_DTYPE),
        compiler_params=_cparams(("parallel",)),
        name="peer_select_scatter",
    )(h3, e1, e2, gate)


def _rope_tables(pos):
    half = RET_DK // 2
    inv = ROPE_BASE ** (-jnp.arange(half, dtype=F32) / half)
    ang = pos.astype(F32)[:, None] * inv[None, :]
    cos = jnp.cos(ang)
    sin = jnp.sin(ang)
    cos_h = jnp.concatenate([cos, cos], axis=-1)
    sin_h = jnp.concatenate([-sin, sin], axis=-1)
    return jnp.tile(cos_h, (1, RET_HEADS)), jnp.tile(sin_h, (1, RET_HEADS))


def _with_zero_stream(state):
    return jnp.concatenate([jnp.zeros_like(state[:1]), state], axis=0)


def _block_diag(x):
    j, g, a, b = x.shape
    eye = jnp.eye(g, dtype=x.dtype)
    return (x[:, :, :, None, :] * eye[None, :, None, :, None]).reshape(j, g * a, g * b)


def kernel(x_prompt, x_sample, state_ret, state_gdn, cache_gdn_conv, state_s5_re, state_s5_im, norm_mix, w_in, w_out, ret_norm, gdn_conv_w, gdn_a_log, gdn_dt_bias, gdn_norm, s5_lam_re, s5_lam_im, s5_log_dt, s5_b_re, s5_b_im, s5_c_re, s5_c_im, s5_d, s5_glu_w, s5_glu_b, norm_ffn, peer_wq, peer_keys, peer_u, peer_v, norm_final):
    bp, tp, d = x_prompt.shape
    nb, ts, _ = x_sample.shape
    depth = w_in.shape[0]
    assert bp == 1 and d == D_MODEL and ts == CHUNK and tp % S5_ROWS == 0
    assert (nb * ts) % S5_ROWS == 0 and CHUNK % S5_RUN == 0
    n_prompt = tp
    n = tp + nb * ts
    n_prompt_chunks = n_prompt // CHUNK
    n_prompt_blocks = n_prompt // S5_ROWS
    runs_per_stream = ts // S5_RUN
    streams_per_block = S5_ROWS // ts

    x = jnp.concatenate([x_prompt.reshape(tp, d), x_sample.reshape(nb * ts, d)], axis=0)
    past_len = 2048
    pos = jnp.concatenate([jnp.arange(tp), jnp.tile(past_len + jnp.arange(ts), nb)])
    cos, sin = _rope_tables(pos)

    ret_out, gdn_out, conv_out, s5r_out, s5i_out = [], [], [], [], []
    for l in range(depth):
        w_l = w_in[l]
        w_main = _mm(jnp.concatenate([w_l[:, :IN_A0], w_l[:, IN_A0 + 2 * GDN_HEADS:]], axis=1))
        w_ab = _mm(jnp.pad(w_l[:, IN_A0:IN_A0 + 2 * GDN_HEADS], ((0, 0), (0, AB_PAD - 2 * GDN_HEADS))))

        h, _, ab = _norm_mm(x, norm_mix[l], w_main, w_ab)

        o_ret, s_ret = _retention(h, cos, sin, _with_zero_stream(state_ret[l]), ret_norm[l], n_prompt_chunks)
        o_gdn, s_conv, s_gdn = _gdn(h, ab, gdn_conv_w[l], gdn_a_log[l], gdn_dt_bias[l], gdn_norm[l],
                                    _with_zero_stream(cache_gdn_conv[l]), _with_zero_stream(state_gdn[l]),
                                    n_prompt_chunks)

        bbt_re, bbt_im, apow_re, apow_im = _s5_prep(s5_lam_re[l], s5_lam_im[l], s5_log_dt[l],
                                                    s5_b_re[l], s5_b_im[l])
        g4 = (S5_NCB, S5_GROUPS // S5_NCB)
        bbw = jnp.concatenate([_block_diag(bbt_re.reshape(*g4, S5_GROUP_CH, S5_STATE)),
                               _block_diag(bbt_im.reshape(*g4, S5_GROUP_CH, S5_STATE))], axis=-1)
        cre = _block_diag(jnp.swapaxes(s5_c_re[l], 1, 2).reshape(*g4, S5_STATE, S5_GROUP_CH))
        cim = _block_diag(jnp.swapaxes(s5_c_im[l], 1, 2).reshape(*g4, S5_STATE, S5_GROUP_CH))

        def run_states(s):
            s = s.reshape(nb // streams_per_block, streams_per_block, 1, S5_LANES)
            s = jnp.broadcast_to(s, (nb // streams_per_block, streams_per_block, runs_per_stream, S5_LANES))
            s = s.reshape(nb // streams_per_block, 8, S5_LANES)
            return jnp.concatenate([jnp.zeros_like(s[:1]), s], axis=0)

        z, end_re, end_im = _s5_scan(h, _mm(bbw), _mm(cre), _mm(cim),
                                     apow_re.reshape(S5_RUN, S5_LANES), apow_im.reshape(S5_RUN, S5_LANES),
                                     run_states(state_s5_re[l]), run_states(state_s5_im[l]), s5_d[l],
                                     n_prompt_blocks, runs_per_stream)
        o_s5 = _glu(z, _mm(s5_glu_w[l]), s5_glu_b[l])

        x = _mix_out(x, o_ret, o_gdn, o_s5, _mm(w_out[l]))

        q, xn, _ = _norm_mm(x, norm_ffn[l], _mm(peer_wq[l]), jnp.zeros((d, AB_PAD), MXU_DTYPE))
        e1, e2, gate = _peer_topk(q, _mm(peer_keys[l].reshape(2 * PEER_HEADS, PEER_NKEYS, PEER_HALF)))
        hd = _matmul(xn, _mm(peer_u[l].T))
        c3 = _peer_select(hd.reshape(n, PEER_NKEYS, PEER_NKEYS), e1, e2, gate)
        x = _res_matmul(x, c3.reshape(n, PEER_EXPERTS), _mm(peer_v[l]))

        def s5_states(e):
            p = e[n_prompt_blocks - 1, 7].reshape(1, S5_GROUPS, S5_STATE)
            s = e[n_prompt_blocks:].reshape(nb, runs_per_stream, S5_LANES)[:, -1]
            return p, s.reshape(nb, S5_GROUPS, S5_STATE)

        ret_out.append(s_ret)
        gdn_out.append(s_gdn)
        conv_out.append(s_conv)
        s5r_out.append(s5_states(end_re))
        s5i_out.append(s5_states(end_im))

    y = _final_norm(x, norm_final)
    y_prompt = y[:tp].reshape(bp, tp, d)
    y_sample = y[tp:].reshape(nb, ts, d)
    split = lambda outs: (jnp.stack([o[:1] for o in outs]), jnp.stack([o[1:] for o in outs]))
    p_ret, s_ret = split(ret_out)
    p_gdn, s_gdn = split(gdn_out)
    p_conv, s_conv = split(conv_out)
    p_s5_re, s_s5_re = (jnp.stack([o[0] for o in s5r_out]), jnp.stack([o[1] for o in s5r_out]))
    p_s5_im, s_s5_im = (jnp.stack([o[0] for o in s5i_out]), jnp.stack([o[1] for o in s5i_out]))
    return (y_prompt, y_sample, p_ret, p_gdn, p_conv, p_s5_re, p_s5_im,
            s_ret, s_gdn, s_conv, s_s5_re, s_s5_im)
```

```python
import functools
import math

import numpy as np
import jax
import jax.numpy as jnp
from jax import lax
from jax.experimental import pallas as pl
from jax.experimental.pallas import tpu as pltpu

F32 = jnp.float32
BF16 = jnp.bfloat16
MXU_DTYPE = BF16
HIGHEST = lax.Precision.HIGHEST

EPS = 1e-6
CHUNK = 64
D_MODEL = 4096
RET_HEADS, RET_DK, RET_DV = 8, 64, 128
RET_QK = RET_HEADS * RET_DK
RET_WIDTH = RET_HEADS * RET_DV
ROPE_BASE = 10000.0
GDN_HEADS, GDN_DK, GDN_DV = 8, 128, 128
GDN_WIDTH = GDN_HEADS * GDN_DV
CONV_W = 4
GDN_CONV_CH = GDN_HEADS * (2 * GDN_DK + GDN_DV)
S5_GROUP_CH, S5_GROUPS, S5_STATE = 16, 128, 64
S5_WIDTH = S5_GROUP_CH * S5_GROUPS
S5_LANES = S5_GROUPS * S5_STATE
PEER_HEADS, PEER_NKEYS, PEER_TOPK = 8, 128, 16
PEER_EXPERTS = PEER_NKEYS * PEER_NKEYS
PEER_HALF = 128
PEER_SLOTS = PEER_HEADS * PEER_TOPK

C_RQ, C_RK, C_RV, C_RG, C_GQKV, C_GG, C_SU = 0, 512, 1024, 2048, 3072, 6144, 7168
H_COLS = 9216
IN_A0 = 2 * RET_QK + 2 * RET_WIDTH + GDN_CONV_CH + GDN_WIDTH
AB_PAD = 128

S5_RUN = 64
S5_ROWS = 8 * S5_RUN
S5_LB = 512
S5_CB = S5_LB // S5_STATE * S5_GROUP_CH
S5_NCB = S5_LANES // S5_LB

VMEM_LIMIT = 56 * 1024 * 1024


def _cparams(sem):
    return pltpu.CompilerParams(dimension_semantics=sem, vmem_limit_bytes=VMEM_LIMIT)


def _dot(a, b, precision=None):
    return jnp.dot(a, b, preferred_element_type=F32, precision=precision)


def _dot_nt(a, b, precision=None):
    return lax.dot_general(a, b, (((1,), (1,)), ((), ())), preferred_element_type=F32,
                           precision=precision)


def _mm(x):
    return x.astype(MXU_DTYPE)


def _sigmoid(x):
    return 1.0 / (1.0 + jnp.exp(-x))


def _silu(x):
    return x * _sigmoid(x)


def _gelu_tanh(x):
    return 0.5 * x * (1.0 + jnp.tanh(math.sqrt(2.0 / math.pi) * (x + 0.044715 * (x * x * x))))


def _norm_mm_kernel(x_ref, g_ref, w_ref, we_ref, o_ref, xn_ref, oe_ref):
    @pl.when(pl.program_id(1) == 0)
    def _():
        x = x_ref[...]
        ms = jnp.mean(x * x, axis=-1, keepdims=True)
        xn = _mm(x * lax.rsqrt(ms + EPS) * g_ref[...])
        xn_ref[...] = xn
        oe_ref[...] = _dot(xn, we_ref[...])

    o_ref[...] = _dot(xn_ref[...], w_ref[...])


def _norm_mm(x, g, w, w_extra, tm=512, tn=1024):
    n, d = x.shape
    nc = w.shape[1]
    ne = w_extra.shape[1]
    return pl.pallas_call(
        _norm_mm_kernel,
        grid=(n // tm, nc // tn),
        in_specs=[pl.BlockSpec((tm, d), lambda i, j: (i, 0)),
                  pl.BlockSpec((1, d), lambda i, j: (0, 0)),
                  pl.BlockSpec((d, tn), lambda i, j: (0, j)),
                  pl.BlockSpec((d, ne), lambda i, j: (0, 0))],
        out_specs=[pl.BlockSpec((tm, tn), lambda i, j: (i, j)),
                   pl.BlockSpec((tm, d), lambda i, j: (i, 0)),
                   pl.BlockSpec((tm, ne), lambda i, j: (i, 0))],
        out_shape=[jax.ShapeDtypeStruct((n, nc), F32),
                   jax.ShapeDtypeStruct((n, d), MXU_DTYPE),
                   jax.ShapeDtypeStruct((n, ne), F32)],
        compiler_params=_cparams(("parallel", "arbitrary")),
        name="norm_proj",
    )(x, g.reshape(1, d), w, w_extra)


def _stream_of_chunk(c, n_prompt_chunks):
    return jnp.where(c < n_prompt_chunks, 0, c - n_prompt_chunks + 1)


def _ret_kernel(n_prompt_chunks, q_ref, k_ref, v_ref, gate_ref, cos_ref, sin_ref, s0_ref, ng_ref,
                o_ref, sout_ref, s_scr):
    c = pl.program_id(0)

    @pl.when(jnp.logical_or(c == 0, c >= n_prompt_chunks))
    def _():
        s_scr[...] = s0_ref[0]

    cos = cos_ref[...]
    sin = sin_ref[...]
    lane = lax.broadcasted_iota(jnp.int32, (CHUNK, RET_QK), 1)
    first_half = (lane % RET_DK) < (RET_DK // 2)

    def rope(x):
        swapped = jnp.where(first_half, pltpu.roll(x, RET_QK - RET_DK // 2, 1),
                            pltpu.roll(x, RET_DK // 2, 1))
        return x * cos + swapped * sin

    q = rope(q_ref[...])
    k = rope(k_ref[...]) * (RET_DK ** -0.5)
    v = v_ref[...]
    gate = gate_ref[...]
    ng = ng_ref[...]

    ti = lax.broadcasted_iota(jnp.int32, (CHUNK, CHUNK), 0)
    tj = lax.broadcasted_iota(jnp.int32, (CHUNK, CHUNK), 1)
    diff = (ti - tj).astype(F32)
    pos = lax.broadcasted_iota(jnp.int32, (CHUNK, 1), 0).astype(F32)

    heads = range(RET_HEADS)
    ld = [math.log(1.0 - 2.0 ** (-5.0 - h)) for h in heads]
    qm = [_mm(q[:, h * RET_DK:(h + 1) * RET_DK]) for h in heads]
    kh = [k[:, h * RET_DK:(h + 1) * RET_DK] for h in heads]
    vm = [_mm(v[:, h * RET_DV:(h + 1) * RET_DV]) for h in heads]
    s = [s_scr[h] for h in heads]
    scores = [_dot_nt(qm[h], _mm(kh[h])) * jnp.where(diff >= 0, jnp.exp(ld[h] * jnp.maximum(diff, 0.0)), 0.0)
              for h in heads]
    cross = [_dot(qm[h], _mm(s[h])) * jnp.exp(ld[h] * (pos + 1.0)) for h in heads]
    o = [_dot(_mm(scores[h]), vm[h]) + cross[h] for h in heads]
    for h in heads:
        kd = kh[h] * jnp.exp(ld[h] * (CHUNK - 1.0 - pos))
        s_scr[h] = s[h] * math.exp(ld[h] * CHUNK) + _dot(_mm(kd.T), vm[h])
    for h in heads:
        mu = jnp.mean(o[h], axis=-1, keepdims=True)
        oc = o[h] - mu
        var = jnp.mean(oc * oc, axis=-1, keepdims=True)
        oh = oc * lax.rsqrt(var + EPS) * ng[:, h * RET_DV:(h + 1) * RET_DV]
        oh = oh * _silu(gate[:, h * RET_DV:(h + 1) * RET_DV])
        o_ref[:, h * RET_DV:(h + 1) * RET_DV] = oh.astype(o_ref.dtype)

    sout_ref[0] = s_scr[...]


def _retention(h, cos, sin, s0, ret_norm, n_prompt_chunks):
    n = h.shape[0]
    nchunks = n // CHUNK
    nstreams = s0.shape[0]
    smap = lambda c: (_stream_of_chunk(c, n_prompt_chunks), 0, 0, 0)
    return pl.pallas_call(
        functools.partial(_ret_kernel, n_prompt_chunks),
        grid=(nchunks,),
        in_specs=[pl.BlockSpec((CHUNK, RET_QK), lambda c: (c, C_RQ // RET_QK)),
                  pl.BlockSpec((CHUNK, RET_QK), lambda c: (c, C_RK // RET_QK)),
                  pl.BlockSpec((CHUNK, RET_WIDTH), lambda c: (c, C_RV // RET_WIDTH)),
                  pl.BlockSpec((CHUNK, RET_WIDTH), lambda c: (c, C_RG // RET_WIDTH)),
                  pl.BlockSpec((CHUNK, RET_QK), lambda c: (c, 0)),
                  pl.BlockSpec((CHUNK, RET_QK), lambda c: (c, 0)),
                  pl.BlockSpec((1, RET_HEADS, RET_DK, RET_DV), smap),
                  pl.BlockSpec((1, RET_WIDTH), lambda c: (0, 0))],
        out_specs=[pl.BlockSpec((CHUNK, RET_WIDTH), lambda c: (c, 0)),
                   pl.BlockSpec((1, RET_HEADS, RET_DK, RET_DV), smap)],
        out_shape=[jax.ShapeDtypeStruct((n, RET_WIDTH), MXU_DTYPE),
                   jax.ShapeDtypeStruct((nstreams, RET_HEADS, RET_DK, RET_DV), F32)],
        scratch_shapes=[pltpu.VMEM((RET_HEADS, RET_DK, RET_DV), F32)],
        compiler_params=_cparams(("arbitrary",)),
        name="retention",
    )(h, h, h, h, cos, sin, s0, ret_norm.reshape(1, RET_WIDTH))


HIST_ROW0 = 8 - (CONV_W - 1)


def _split(x):
    hi = _mm(x)
    return hi, _mm(x - hi.astype(F32))


def _dot_split(a, b):
    return _dot(a[0], b[0]) + (_dot(a[0], b[1]) + _dot(a[1], b[0]))


def _unit_lower_inverses(mats):
    n = mats[0].shape[0]
    ti = lax.broadcasted_iota(jnp.int32, (n, n), 0)
    tj = lax.broadcasted_iota(jnp.int32, (n, n), 1)
    eye = (ti == tj).astype(F32)
    inv = [eye - a for a in mats]
    ps = [_split(a) for a in mats]
    k = 2
    while k < n:
        ps = [_split(_dot_split(x, x)) for x in ps]
        inv = [i + _dot_split(_split(i), x) for i, x in zip(inv, ps)]
        k *= 2
    return inv


def _gdn_kernel(n_prompt_chunks, qkv_ref, gate_ref, ab_ref, cw_ref, alog_ref, dtb_ref, ng_ref,
                hist0_ref, s0_ref, o_ref, hist_out_ref, sout_ref, xin_scr, s_scr):
    c = pl.program_id(0)

    @pl.when(jnp.logical_or(c == 0, c >= n_prompt_chunks))
    def _():
        s_scr[...] = s0_ref[0]
        xin_scr[HIST_ROW0:8, :] = hist0_ref[0]

    xin_scr[8:8 + CHUNK, :] = qkv_ref[...]
    cw = cw_ref[...]
    y = xin_scr[HIST_ROW0:HIST_ROW0 + CHUNK, :] * cw[0:1, :]
    for w in range(1, CONV_W):
        y = y + xin_scr[HIST_ROW0 + w:HIST_ROW0 + w + CHUNK, :] * cw[w:w + 1, :]
    y = _silu(y)
    new_hist = xin_scr[8 + CHUNK - (CONV_W - 1):8 + CHUNK, :]
    hist_out_ref[0] = new_hist
    xin_scr[HIST_ROW0:8, :] = new_hist

    ab = ab_ref[...]
    x = ab + dtb_ref[...]
    softplus = jnp.maximum(x, 0.0) + jnp.log1p(jnp.exp(-jnp.abs(x)))
    g_all = -jnp.exp(alog_ref[...]) * softplus
    beta_all = _sigmoid(ab)

    ti = lax.broadcasted_iota(jnp.int32, (CHUNK, CHUNK), 0)
    tj = lax.broadcasted_iota(jnp.int32, (CHUNK, CHUNK), 1)
    lower = ti >= tj
    strict = ti > tj
    gc_all = _dot(lower.astype(F32), g_all, HIGHEST)
    gc_t = gc_all.T
    gate = gate_ref[...]
    ng = ng_ref[...]

    heads = range(GDN_HEADS)
    gcol = [gc_all[:, h:h + 1] for h in heads]
    decay = [jnp.where(lower, jnp.exp(jnp.where(lower, gcol[h] - gc_t[h:h + 1, :], 0.0)), 0.0)
             for h in heads]
    beta = [beta_all[:, GDN_HEADS + h:GDN_HEADS + h + 1] for h in heads]
    egc = [jnp.exp(g) for g in gcol]
    g_last = [gc_all[CHUNK - 1:CHUNK, h:h + 1] for h in heads]

    def unit(xc):
        return xc * lax.rsqrt(jnp.sum(xc * xc, axis=-1, keepdims=True) + EPS)

    qh = [unit(y[:, h * GDN_DK:(h + 1) * GDN_DK]) * (GDN_DK ** -0.5) for h in heads]
    kh = [unit(y[:, GDN_WIDTH + h * GDN_DK:GDN_WIDTH + (h + 1) * GDN_DK]) for h in heads]
    vc = [y[:, 2 * GDN_WIDTH + h * GDN_DV:2 * GDN_WIDTH + (h + 1) * GDN_DV] for h in heads]
    kb = [kh[h] * beta[h] for h in heads]
    khm = [_mm(k) for k in kh]
    a_mat = [jnp.where(strict, _dot_nt(_mm(kb[h]), khm[h]) * decay[h], 0.0) for h in heads]
    attn = [_mm(_dot_nt(_mm(qh[h]), khm[h]) * decay[h]) for h in heads]
    t_inv = _unit_lower_inverses(a_mat)
    sol = [_dot_split(_split(t_inv[h]),
                      _split(jnp.concatenate([vc[h] * beta[h], kb[h] * egc[h]], axis=-1)))
           for h in heads]
    s = [s_scr[h] for h in heads]
    sm = [_mm(x) for x in s]
    v_new = [sol[h][:, :GDN_DV] - _dot(_mm(sol[h][:, GDN_DV:]), sm[h]) for h in heads]
    vm = [_mm(v) for v in v_new]
    o = [_dot(_mm(qh[h] * egc[h]), sm[h]) + _dot(attn[h], vm[h]) for h in heads]
    for h in heads:
        kd = kh[h] * jnp.exp(g_last[h] - gcol[h])
        s_scr[h] = s[h] * jnp.exp(g_last[h]) + _dot(_mm(kd.T), vm[h])
    for h in heads:
        oh = o[h] * lax.rsqrt(jnp.mean(o[h] * o[h], axis=-1, keepdims=True) + EPS) * ng
        oh = oh * _silu(gate[:, h * GDN_DV:(h + 1) * GDN_DV])
        o_ref[:, h * GDN_DV:(h + 1) * GDN_DV] = oh.astype(o_ref.dtype)

    sout_ref[0] = s_scr[...]


def _gdn(h, ab, conv_w, a_log, dt_bias, gdn_norm, hist0, s0, n_prompt_chunks):
    n = h.shape[0]
    nchunks = n // CHUNK
    nstreams = s0.shape[0]
    smap4 = lambda c: (_stream_of_chunk(c, n_prompt_chunks), 0, 0, 0)
    smap3 = lambda c: (_stream_of_chunk(c, n_prompt_chunks), 0, 0)
    alog_row = jnp.zeros((1, AB_PAD), F32).at[0, :GDN_HEADS].set(a_log)
    dtb_row = jnp.zeros((1, AB_PAD), F32).at[0, :GDN_HEADS].set(dt_bias)
    return pl.pallas_call(
        functools.partial(_gdn_kernel, n_prompt_chunks),
        grid=(nchunks,),
        in_specs=[pl.BlockSpec((CHUNK, GDN_CONV_CH), lambda c: (c, C_GQKV // GDN_CONV_CH)),
                  pl.BlockSpec((CHUNK, GDN_WIDTH), lambda c: (c, C_GG // GDN_WIDTH)),
                  pl.BlockSpec((CHUNK, AB_PAD), lambda c: (c, 0)),
                  pl.BlockSpec((CONV_W, GDN_CONV_CH), lambda c: (0, 0)),
                  pl.BlockSpec((1, AB_PAD), lambda c: (0, 0)),
                  pl.BlockSpec((1, AB_PAD), lambda c: (0, 0)),
                  pl.BlockSpec((1, GDN_DV), lambda c: (0, 0)),
                  pl.BlockSpec((1, CONV_W - 1, GDN_CONV_CH), smap3),
                  pl.BlockSpec((1, GDN_HEADS, GDN_DK, GDN_DV), smap4)],
        out_specs=[pl.BlockSpec((CHUNK, GDN_WIDTH), lambda c: (c, 0)),
                   pl.BlockSpec((1, CONV_W - 1, GDN_CONV_CH), smap3),
                   pl.BlockSpec((1, GDN_HEADS, GDN_DK, GDN_DV), smap4)],
        out_shape=[jax.ShapeDtypeStruct((n, GDN_WIDTH), MXU_DTYPE),
                   jax.ShapeDtypeStruct((nstreams, CONV_W - 1, GDN_CONV_CH), F32),
                   jax.ShapeDtypeStruct((nstreams, GDN_HEADS, GDN_DK, GDN_DV), F32)],
        scratch_shapes=[pltpu.VMEM((8 + CHUNK, GDN_CONV_CH), F32),
                        pltpu.VMEM((GDN_HEADS, GDN_DK, GDN_DV), F32)],
        compiler_params=_cparams(("arbitrary",)),
        name="gated_delta",
    )(h, h, ab, conv_w, alog_row, dtb_row, gdn_norm.reshape(1, GDN_DV), hist0, s0)


def _s5_prep_kernel(lre_ref, lim_ref, ldt_ref, btr_ref, bti_ref,
                    bbr_ref, bbi_ref, apr_ref, api_ref):
    lam_re = lre_ref[...]
    lam_im = lim_ref[...]
    dt = jnp.exp(ldt_ref[...])
    mag = jnp.exp(lam_re * dt)
    a_re = mag * jnp.cos(lam_im * dt)
    a_im = mag * jnp.sin(lam_im * dt)
    den = lam_re * lam_re + lam_im * lam_im
    f_re = ((a_re - 1.0) * lam_re + a_im * lam_im) / den
    f_im = (a_im * lam_re - (a_re - 1.0) * lam_im) / den
    btr = btr_ref[...]
    bti = bti_ref[...]
    bbr_ref[...] = f_re * btr - f_im * bti
    bbi_ref[...] = f_re * bti + f_im * btr
    p_re, p_im = a_re, a_im
    apr_ref[0] = p_re
    api_ref[0] = p_im
    for v in range(1, S5_RUN):
        p_re, p_im = p_re * a_re - p_im * a_im, p_re * a_im + p_im * a_re
        apr_ref[v] = p_re
        api_ref[v] = p_im


def _s5_prep(lam_re, lam_im, log_dt, b_re, b_im):
    g, p = lam_re.shape
    nch = b_re.shape[-1]
    return pl.pallas_call(
        _s5_prep_kernel,
        out_shape=[jax.ShapeDtypeStruct((g, nch, p), F32), jax.ShapeDtypeStruct((g, nch, p), F32),
                   jax.ShapeDtypeStruct((S5_RUN, g, 1, p), F32),
                   jax.ShapeDtypeStruct((S5_RUN, g, 1, p), F32)],
        name="s5_discretise",
    )(lam_re.reshape(g, 1, p), lam_im.reshape(g, 1, p), log_dt.reshape(g, 1, 1),
      jnp.swapaxes(b_re, 1, 2), jnp.swapaxes(b_im, 1, 2))


def _s5_kernel(n_prompt_blocks, runs_per_stream, u_ref, bbw_ref, cre_ref, cim_ref, apr_ref, api_ref,
               x0r_ref, x0i_ref, d_ref, z_ref, endr_ref, endi_ref, xr_scr, xi_scr, car_scr, cai_scr):
    b = pl.program_id(1)
    u_perm = jnp.concatenate([u_ref[pl.ds(v, 8, stride=S5_RUN), :] for v in range(S5_RUN)], axis=0)
    e = _dot(_mm(u_perm), bbw_ref[0])
    a_re = jnp.broadcast_to(apr_ref[0:1, :], (8, S5_LB))
    a_im = jnp.broadcast_to(api_ref[0:1, :], (8, S5_LB))

    xr = e[0:8, :S5_LB]
    xi = e[0:8, S5_LB:]
    xr_scr[0:8, :] = xr
    xi_scr[0:8, :] = xi
    for v in range(1, S5_RUN):
        er = e[8 * v:8 * v + 8, :S5_LB]
        ei = e[8 * v:8 * v + 8, S5_LB:]
        xr, xi = a_re * xr - a_im * xi + er, a_re * xi + a_im * xr + ei
        xr_scr[8 * v:8 * v + 8, :] = xr
        xi_scr[8 * v:8 * v + 8, :] = xi

    al_re = apr_ref[S5_RUN - 1:S5_RUN, :]
    al_im = api_ref[S5_RUN - 1:S5_RUN, :]
    x0r = x0r_ref[0]
    x0i = x0i_ref[0]
    is_sample = b >= n_prompt_blocks
    prev_r = car_scr[...]
    prev_i = cai_scr[...]
    cin_r, cin_i, end_r, end_i = [], [], [], []
    for r in range(8):
        if r % runs_per_stream == 0:
            reset = jnp.logical_or(is_sample, b == 0) if r == 0 else is_sample
        else:
            reset = None
        if reset is None:
            cr, ci = prev_r, prev_i
        else:
            cr = jnp.where(reset, x0r[r:r + 1, :], prev_r)
            ci = jnp.where(reset, x0i[r:r + 1, :], prev_i)
        prev_r = al_re * cr - al_im * ci + xr[r:r + 1, :]
        prev_i = al_re * ci + al_im * cr + xi[r:r + 1, :]
        cin_r.append(cr)
        cin_i.append(ci)
        end_r.append(prev_r)
        end_i.append(prev_i)
    car_scr[...] = prev_r
    cai_scr[...] = prev_i
    endr_ref[0] = jnp.concatenate(end_r, axis=0)
    endi_ref[0] = jnp.concatenate(end_i, axis=0)
    cr = jnp.concatenate(cin_r, axis=0)
    ci = jnp.concatenate(cin_i, axis=0)

    for v in range(S5_RUN):
        pr = jnp.broadcast_to(apr_ref[v:v + 1, :], (8, S5_LB))
        pi = jnp.broadcast_to(api_ref[v:v + 1, :], (8, S5_LB))
        xr_scr[8 * v:8 * v + 8, :] = xr_scr[8 * v:8 * v + 8, :] + (pr * cr - pi * ci)
        xi_scr[8 * v:8 * v + 8, :] = xi_scr[8 * v:8 * v + 8, :] + (pr * ci + pi * cr)

    y = _dot(_mm(xr_scr[...]), cre_ref[0]) - _dot(_mm(xi_scr[...]), cim_ref[0])
    z = _gelu_tanh(y + d_ref[...] * u_perm)
    for v in range(S5_RUN):
        z_ref[pl.ds(v, 8, stride=S5_RUN), :] = z[8 * v:8 * v + 8, :]


def _s5_scan(h, bbw, cre, cim, apow_re, apow_im, x0_re, x0_im, d_skip, n_prompt_blocks, runs_per_stream):
    n = h.shape[0]
    nblocks = n // S5_ROWS
    xmap = lambda j, b: (jnp.where(b < n_prompt_blocks, 0, b - n_prompt_blocks + 1), 0, j)
    return pl.pallas_call(
        functools.partial(_s5_kernel, n_prompt_blocks, runs_per_stream),
        grid=(S5_NCB, nblocks),
        in_specs=[pl.BlockSpec((S5_ROWS, S5_CB), lambda j, b: (b, C_SU // S5_CB + j)),
                  pl.BlockSpec((1, S5_CB, 2 * S5_LB), lambda j, b: (j, 0, 0)),
                  pl.BlockSpec((1, S5_LB, S5_CB), lambda j, b: (j, 0, 0)),
                  pl.BlockSpec((1, S5_LB, S5_CB), lambda j, b: (j, 0, 0)),
                  pl.BlockSpec((S5_RUN, S5_LB), lambda j, b: (0, j)),
                  pl.BlockSpec((S5_RUN, S5_LB), lambda j, b: (0, j)),
                  pl.BlockSpec((1, 8, S5_LB), xmap),
                  pl.BlockSpec((1, 8, S5_LB), xmap),
                  pl.BlockSpec((1, S5_CB), lambda j, b: (0, j))],
        out_specs=[pl.BlockSpec((S5_ROWS, S5_CB), lambda j, b: (b, j)),
                   pl.BlockSpec((1, 8, S5_LB), lambda j, b: (b, 0, j)),
                   pl.BlockSpec((1, 8, S5_LB), lambda j, b: (b, 0, j))],
        out_shape=[jax.ShapeDtypeStruct((n, S5_WIDTH), F32),
                   jax.ShapeDtypeStruct((nblocks, 8, S5_LANES), F32),
                   jax.ShapeDtypeStruct((nblocks, 8, S5_LANES), F32)],
        scratch_shapes=[pltpu.VMEM((S5_ROWS, S5_LB), F32), pltpu.VMEM((S5_ROWS, S5_LB), F32),
                        pltpu.VMEM((1, S5_LB), F32), pltpu.VMEM((1, S5_LB), F32)],
        compiler_params=_cparams(("parallel", "arbitrary")),
        name="s5_scan",
    )(h, bbw, cre, cim, apow_re, apow_im, x0_re, x0_im, d_skip.reshape(1, S5_WIDTH))


def _glu_kernel(zr_ref, w_ref, b_ref, zc_ref, o_ref):
    lin = _dot(_mm(zr_ref[...]), w_ref[...]) + b_ref[...]
    o_ref[...] = (zc_ref[...] * _sigmoid(lin)).astype(o_ref.dtype)


def _glu(z, w, bias, tm=512, tn=512):
    n, k = z.shape
    nc = w.shape[1]
    return pl.pallas_call(
        _glu_kernel,
        grid=(n // tm, nc // tn),
        in_specs=[pl.BlockSpec((tm, k), lambda i, j: (i, 0)),
                  pl.BlockSpec((k, tn), lambda i, j: (0, j)),
                  pl.BlockSpec((1, tn), lambda i, j: (0, j)),
                  pl.BlockSpec((tm, tn), lambda i, j: (i, j))],
        out_specs=pl.BlockSpec((tm, tn), lambda i, j: (i, j)),
        out_shape=jax.ShapeDtypeStruct((n, nc), MXU_DTYPE),
        compiler_params=_cparams(("parallel", "parallel")),
        name="s5_glu",
    )(z, w, bias.reshape(1, nc), z)


def _mix_out_kernel(x_ref, a_ref, b_ref, c_ref, w_ref, o_ref):
    ka = a_ref.shape[1]
    kb = b_ref.shape[1]
    acc = _dot(a_ref[...], w_ref[0:ka, :])
    acc = acc + _dot(b_ref[...], w_ref[ka:ka + kb, :])
    acc = acc + _dot(c_ref[...], w_ref[ka + kb:, :])
    o_ref[...] = x_ref[...] + acc


def _mix_out(x, o_ret, o_gdn, o_s5, w, tm=512, tn=1024):
    n, d = x.shape
    k = w.shape[0]
    return pl.pallas_call(
        _mix_out_kernel,
        grid=(n // tm, d // tn),
        in_specs=[pl.BlockSpec((tm, tn), lambda i, j: (i, j)),
                  pl.BlockSpec((tm, o_ret.shape[1]), lambda i, j: (i, 0)),
                  pl.BlockSpec((tm, o_gdn.shape[1]), lambda i, j: (i, 0)),
                  pl.BlockSpec((tm, o_s5.shape[1]), lambda i, j: (i, 0)),
                  pl.BlockSpec((k, tn), lambda i, j: (0, j))],
        out_specs=pl.BlockSpec((tm, tn), lambda i, j: (i, j)),
        out_shape=jax.ShapeDtypeStruct((n, d), F32),
        compiler_params=_cparams(("parallel", "parallel")),
        name="mix_out_proj",
    )(x, o_ret, o_gdn, o_s5, w)


def _peer_hidden_kernel(a_ref, w_ref, o_ref):
    r = _dot(a_ref[...], w_ref[...])
    for e in range(o_ref.shape[1]):
        o_ref[:, e, :] = r[:, e * PEER_NKEYS:(e + 1) * PEER_NKEYS].astype(o_ref.dtype)


def _peer_hidden(a, w, tm=512, tn=1024):
    n, k = a.shape
    te = tn // PEER_NKEYS
    return pl.pallas_call(
        _peer_hidden_kernel,
        grid=(n // tm, PEER_EXPERTS // tn),
        in_specs=[pl.BlockSpec((tm, k), lambda i, j: (i, 0)),
                  pl.BlockSpec((k, tn), lambda i, j: (0, j))],
        out_specs=pl.BlockSpec((tm, te, PEER_NKEYS), lambda i, j: (i, j, 0)),
        out_shape=jax.ShapeDtypeStruct((n, PEER_NKEYS, PEER_NKEYS), MXU_DTYPE),
        compiler_params=_cparams(("parallel", "parallel")),
        name="peer_hidden_dense",
    )(a, w)


def _peer_out_kernel(x_ref, c3_ref, w_ref, o_ref):
    @pl.when(pl.program_id(2) == 0)
    def _():
        o_ref[...] = x_ref[...]

    c = jnp.concatenate([c3_ref[:, e, :] for e in range(c3_ref.shape[1])], axis=-1)
    o_ref[...] += _dot(c, w_ref[...])


def _peer_out(x, c3, w, tm=512, tn=1024, tk=4096):
    n, d = x.shape
    te = tk // PEER_NKEYS
    return pl.pallas_call(
        _peer_out_kernel,
        grid=(n // tm, d // tn, PEER_EXPERTS // tk),
        in_specs=[pl.BlockSpec((tm, tn), lambda i, j, kk: (i, j)),
                  pl.BlockSpec((tm, te, PEER_NKEYS), lambda i, j, kk: (i, kk, 0)),
                  pl.BlockSpec((tk, tn), lambda i, j, kk: (kk, j))],
        out_specs=pl.BlockSpec((tm, tn), lambda i, j, kk: (i, j)),
        out_shape=jax.ShapeDtypeStruct((n, d), F32),
        compiler_params=_cparams(("parallel", "parallel", "arbitrary")),
        name="peer_out_dense",
    )(x, c3, w)


def _final_norm_kernel(n_prompt_tiles, x_ref, g_ref, op_ref, os_ref):
    x = x_ref[...]
    ms = jnp.mean(x * x, axis=-1, keepdims=True)
    y = x * lax.rsqrt(ms + EPS) * g_ref[...]
    i = pl.program_id(0)

    @pl.when(i < n_prompt_tiles)
    def _():
        op_ref[...] = y

    @pl.when(i >= n_prompt_tiles)
    def _():
        os_ref[...] = y


def _final_norm(x, g, n_prompt, tm=512):
    n, d = x.shape
    npt = n_prompt // tm
    return pl.pallas_call(
        functools.partial(_final_norm_kernel, npt),
        grid=(n // tm,),
        in_specs=[pl.BlockSpec((tm, d), lambda i: (i, 0)), pl.BlockSpec((1, d), lambda i: (0, 0))],
        out_specs=[pl.BlockSpec((tm, d), lambda i: (jnp.minimum(i, npt - 1), 0)),
                   pl.BlockSpec((tm, d), lambda i: (jnp.maximum(i - npt, 0), 0))],
        out_shape=[jax.ShapeDtypeStruct((n_prompt, d), F32),
                   jax.ShapeDtypeStruct((n - n_prompt, d), F32)],
        compiler_params=_cparams(("arbitrary",)),
        name="final_norm",
    )(x, g.reshape(1, d))


NEG_INF = float("-inf")


def _top16_rows(s, ids, val_ref, id_ref):
    big = float(2 ** 20)
    for r in range(PEER_TOPK):
        m = jnp.max(s, axis=0, keepdims=True)
        i = jnp.min(jnp.where(s == m, ids, big), axis=0, keepdims=True)
        s = jnp.where(ids == i, NEG_INF, s)
        val_ref[r:r + 1, :] = m
        id_ref[r:r + 1, :] = i
    return val_ref[...], id_ref[...]


def _peer_topk_kernel(q_ref, keys_ref, e1_ref, e2_ref, gate_ref,
                      v0_scr, i0_scr, v1_scr, i1_scr, vb_scr, ib_scr, e1_scr, e2_scr, g_scr):
    tt = q_ref.shape[0]
    key_ids = lax.broadcasted_iota(jnp.int32, (PEER_NKEYS, tt), 0).astype(F32)
    row16 = lax.broadcasted_iota(jnp.int32, (PEER_TOPK, tt), 0).astype(F32)
    row8 = lax.broadcasted_iota(jnp.int32, (8, tt), 0).astype(F32)
    cand_ids = [row16]
    cand_ok = [row16 >= 0.0]
    for i in range(1, 8):
        cand_ids.append(row8 + 16.0 * i)
        cand_ok.append(row8 < float(PEER_TOPK // (i + 1)))
    cand_ids.append(16.0 * (row8 + 8.0))
    cand_ok.append(row8 >= 0.0)
    cand_ids = jnp.concatenate(cand_ids, axis=0)
    cand_ok = jnp.concatenate(cand_ok, axis=0)

    for h in range(PEER_HEADS):
        col = 2 * h * PEER_HALF
        s0 = _dot_nt(keys_ref[2 * h], _mm(q_ref[:, col:col + PEER_HALF]))
        t0, ti0 = _top16_rows(s0, key_ids, v0_scr, i0_scr)
        s1 = _dot_nt(keys_ref[2 * h + 1], _mm(q_ref[:, col + PEER_HALF:col + 2 * PEER_HALF]))
        t1, ti1 = _top16_rows(s1, key_ids, v1_scr, i1_scr)
        cand = [t0[0:1] + t1]
        for i in range(1, 8):
            cand.append(t0[i:i + 1] + t1[0:8])
        cand.append(t0[8:16] + t1[0:1])
        cand = jnp.where(cand_ok, jnp.concatenate(cand, axis=0), NEG_INF)
        best, flat = _top16_rows(cand, cand_ids, vb_scr, ib_scr)
        flat = flat.astype(jnp.int32)
        i_sel = flat >> 4
        j_sel = flat & (PEER_TOPK - 1)
        e1 = jnp.zeros_like(best)
        e2 = jnp.zeros_like(best)
        for r in range(PEER_TOPK):
            e1 = jnp.where(i_sel == r, ti0[r:r + 1], e1)
            e2 = jnp.where(j_sel == r, ti1[r:r + 1], e2)
        p = jnp.exp(best - best[0:1])
        gate = p / jnp.sum(p, axis=0, keepdims=True)
        e1_scr[h * PEER_TOPK:(h + 1) * PEER_TOPK, :] = e1
        e2_scr[h * PEER_TOPK:(h + 1) * PEER_TOPK, :] = e2
        g_scr[h * PEER_TOPK:(h + 1) * PEER_TOPK, :] = gate
    e1_ref[...] = e1_scr[...].T
    e2_ref[...] = e2_scr[...].T
    gate_ref[...] = g_scr[...].T


def _peer_topk(q, keys, tt=256):
    n = q.shape[0]
    spec = pl.BlockSpec((tt, PEER_SLOTS), lambda i: (i, 0))
    rows16 = pltpu.VMEM((PEER_TOPK, tt), F32)
    slots = pltpu.VMEM((PEER_SLOTS, tt), F32)
    return pl.pallas_call(
        _peer_topk_kernel,
        grid=(n // tt,),
        in_specs=[pl.BlockSpec((tt, q.shape[1]), lambda i: (i, 0)),
                  pl.BlockSpec(keys.shape, lambda i: (0, 0, 0))],
        out_specs=[spec, spec, spec],
        out_shape=[jax.ShapeDtypeStruct((n, PEER_SLOTS), F32)] * 3,
        scratch_shapes=[rows16] * 6 + [slots] * 3,
        compiler_params=_cparams(("parallel",)),
        name="peer_topk",
    )(q, keys)


def _peer_select_kernel(h3_ref, e1_ref, e2_ref, gate_ref, c3_ref):
    tt = h3_ref.shape[0]
    sub = lax.broadcasted_iota(jnp.int32, (PEER_NKEYS, PEER_SLOTS), 0).astype(F32)

    group = 8

    def body(g, carry):
        toks = [g * group + i for i in range(group)]
        on1 = [sub == e1_ref[pl.ds(t, 1), :] for t in toks]
        b2 = [jnp.where(sub == e2_ref[pl.ds(t, 1), :], 1.0, 0.0).astype(MXU_DTYPE) for t in toks]
        m = [_dot(h3_ref[t], b) for t, b in zip(toks, b2)]
        hid = [jnp.sum(jnp.where(o, x, 0.0), axis=0, keepdims=True) for o, x in zip(on1, m)]
        coef = [_gelu_tanh(x) * gate_ref[pl.ds(t, 1), :] for t, x in zip(toks, hid)]
        a1 = [jnp.where(o, c, 0.0).astype(MXU_DTYPE) for o, c in zip(on1, coef)]
        for t, a, b in zip(toks, a1, b2):
            c3_ref[t] = _dot_nt(a, b).astype(c3_ref.dtype)
        return carry

    lax.fori_loop(0, tt // group, body, 0)


def _peer_select(h3, e1, e2, gate, tt=64):
    n = h3.shape[0]
    spec = pl.BlockSpec((tt, PEER_SLOTS), lambda i: (i, 0))
    return pl.pallas_call(
        _peer_select_kernel,
        grid=(n // tt,),
        in_specs=[pl.BlockSpec((tt, PEER_NKEYS, PEER_NKEYS), lambda i: (i, 0, 0)), spec, spec, spec],
        out_specs=pl.BlockSpec((tt, PEER_NKEYS, PEER_NKEYS), lambda i: (i, 0, 0)),
        out_shape=jax.ShapeDtypeStruct((n, PEER_NKEYS, PEER_NKEYS), MXU_DTYPE),
        compiler_params=_cparams(("parallel",)),
        name="peer_select_scatter",
    )(h3, e1, e2, gate)


def _rope_tables(pos):
    half = RET_DK // 2
    inv = ROPE_BASE ** (-jnp.arange(half, dtype=F32) / half)
    ang = pos.astype(F32)[:, None] * inv[None, :]
    cos = jnp.cos(ang)
    sin = jnp.sin(ang)
    cos_h = jnp.concatenate([cos, cos], axis=-1)
    sin_h = jnp.concatenate([-sin, sin], axis=-1)
    return jnp.tile(cos_h, (1, RET_HEADS)), jnp.tile(sin_h, (1, RET_HEADS))


def _with_zero_stream(state):
    return jnp.concatenate([jnp.zeros_like(state[:1]), state], axis=0)


def _block_diag(x):
    j, g, a, b = x.shape
    eye = jnp.eye(g, dtype=x.dtype)
    return (x[:, :, :, None, :] * eye[None, :, None, :, None]).reshape(j, g * a, g * b)


def kernel(x_prompt, x_sample, state_ret, state_gdn, cache_gdn_conv, state_s5_re, state_s5_im, norm_mix, w_in, w_out, ret_norm, gdn_conv_w, gdn_a_log, gdn_dt_bias, gdn_norm, s5_lam_re, s5_lam_im, s5_log_dt, s5_b_re, s5_b_im, s5_c_re, s5_c_im, s5_d, s5_glu_w, s5_glu_b, norm_ffn, peer_wq, peer_keys, peer_u, peer_v, norm_final):
    bp, tp, d = x_prompt.shape
    nb, ts, _ = x_sample.shape
    depth = w_in.shape[0]
    assert bp == 1 and d == D_MODEL and ts == CHUNK and tp % S5_ROWS == 0
    assert (nb * ts) % S5_ROWS == 0 and CHUNK % S5_RUN == 0
    n_prompt = tp
    n = tp + nb * ts
    n_prompt_chunks = n_prompt // CHUNK
    n_prompt_blocks = n_prompt // S5_ROWS
    runs_per_stream = ts // S5_RUN
    streams_per_block = S5_ROWS // ts

    x = jnp.concatenate([x_prompt.reshape(tp, d), x_sample.reshape(nb * ts, d)], axis=0)
    past_len = 2048
    pos = jnp.concatenate([jnp.arange(tp), jnp.tile(past_len + jnp.arange(ts), nb)])
    cos, sin = _rope_tables(pos)

    ret_out, gdn_out, conv_out, s5r_out, s5i_out = [], [], [], [], []
    for l in range(depth):
        w_l = w_in[l]
        w_main = _mm(jnp.concatenate([w_l[:, :IN_A0], w_l[:, IN_A0 + 2 * GDN_HEADS:]], axis=1))
        w_ab = _mm(jnp.pad(w_l[:, IN_A0:IN_A0 + 2 * GDN_HEADS], ((0, 0), (0, AB_PAD - 2 * GDN_HEADS))))

        h, _, ab = _norm_mm(x, norm_mix[l], w_main, w_ab)

        o_ret, s_ret = _retention(h, cos, sin, _with_zero_stream(state_ret[l]), ret_norm[l], n_prompt_chunks)
        o_gdn, s_conv, s_gdn = _gdn(h, ab, gdn_conv_w[l], gdn_a_log[l], gdn_dt_bias[l], gdn_norm[l],
                                    _with_zero_stream(cache_gdn_conv[l]), _with_zero_stream(state_gdn[l]),
                                    n_prompt_chunks)

        bbt_re, bbt_im, apow_re, apow_im = _s5_prep(s5_lam_re[l], s5_lam_im[l], s5_log_dt[l],
                                                    s5_b_re[l], s5_b_im[l])
        g4 = (S5_NCB, S5_GROUPS // S5_NCB)
        bbw = jnp.concatenate([_block_diag(bbt_re.reshape(*g4, S5_GROUP_CH, S5_STATE)),
                               _block_diag(bbt_im.reshape(*g4, S5_GROUP_CH, S5_STATE))], axis=-1)
        cre = _block_diag(jnp.swapaxes(s5_c_re[l], 1, 2).reshape(*g4, S5_STATE, S5_GROUP_CH))
        cim = _block_diag(jnp.swapaxes(s5_c_im[l], 1, 2).reshape(*g4, S5_STATE, S5_GROUP_CH))

        def run_states(s):
            s = s.reshape(nb // streams_per_block, streams_per_block, 1, S5_LANES)
            s = jnp.broadcast_to(s, (nb // streams_per_block, streams_per_block, runs_per_stream, S5_LANES))
            s = s.reshape(nb // streams_per_block, 8, S5_LANES)
            return jnp.concatenate([jnp.zeros_like(s[:1]), s], axis=0)

        z, end_re, end_im = _s5_scan(h, _mm(bbw), _mm(cre), _mm(cim),
                                     apow_re.reshape(S5_RUN, S5_LANES), apow_im.reshape(S5_RUN, S5_LANES),
                                     run_states(state_s5_re[l]), run_states(state_s5_im[l]), s5_d[l],
                                     n_prompt_blocks, runs_per_stream)
        o_s5 = _glu(z, _mm(s5_glu_w[l]), s5_glu_b[l])

        x = _mix_out(x, o_ret, o_gdn, o_s5, _mm(w_out[l]))

        q, xn, _ = _norm_mm(x, norm_ffn[l], _mm(peer_wq[l]), jnp.zeros((d, AB_PAD), MXU_DTYPE))
        e1, e2, gate = _peer_topk(q, _mm(peer_keys[l].reshape(2 * PEER_HEADS, PEER_NKEYS, PEER_HALF)))
        h3 = _peer_hidden(xn, _mm(peer_u[l].T))
        c3 = _peer_select(h3, e1, e2, gate)
        x = _peer_out(x, c3, _mm(peer_v[l]))

        def s5_states(e):
            p = e[n_prompt_blocks - 1, 7].reshape(1, S5_GROUPS, S5_STATE)
            s = e[n_prompt_blocks:].reshape(nb, runs_per_stream, S5_LANES)[:, -1]
            return p, s.reshape(nb, S5_GROUPS, S5_STATE)

        ret_out.append(s_ret)
        gdn_out.append(s_gdn)
        conv_out.append(s_conv)
        s5r_out.append(s5_states(end_re))
        s5i_out.append(s5_states(end_im))

    y_prompt, y_sample = _final_norm(x, norm_final, n_prompt)
    y_prompt = y_prompt.reshape(bp, tp, d)
    y_sample = y_sample.reshape(nb, ts, d)
    split = lambda outs: (jnp.stack([o[:1] for o in outs]), jnp.stack([o[1:] for o in outs]))
    p_ret, s_ret = split(ret_out)
    p_gdn, s_gdn = split(gdn_out)
    p_conv, s_conv = split(conv_out)
    p_s5_re, s_s5_re = (jnp.stack([o[0] for o in s5r_out]), jnp.stack([o[1] for o in s5r_out]))
    p_s5_im, s_s5_im = (jnp.stack([o[0] for o in s5i_out]), jnp.stack([o[1] for o in s5i_out]))
    return (y_prompt, y_sample, p_ret, p_gdn, p_conv, p_s5_re, p_s5_im,
            s_ret, s_gdn, s_conv, s_s5_re, s_s5_im)
```

```python
import functools
import math

import numpy as np
import jax
import jax.numpy as jnp
from jax import lax
from jax.experimental import pallas as pl
from jax.experimental.pallas import tpu as pltpu

F32 = jnp.float32
BF16 = jnp.bfloat16
MXU_DTYPE = BF16
HIGHEST = lax.Precision.HIGHEST

EPS = 1e-6
CHUNK = 64
D_MODEL = 4096
RET_HEADS, RET_DK, RET_DV = 8, 64, 128
RET_QK = RET_HEADS * RET_DK
RET_WIDTH = RET_HEADS * RET_DV
ROPE_BASE = 10000.0
GDN_HEADS, GDN_DK, GDN_DV = 8, 128, 128
GDN_WIDTH = GDN_HEADS * GDN_DV
CONV_W = 4
GDN_CONV_CH = GDN_HEADS * (2 * GDN_DK + GDN_DV)
S5_GROUP_CH, S5_GROUPS, S5_STATE = 16, 128, 64
S5_WIDTH = S5_GROUP_CH * S5_GROUPS
S5_LANES = S5_GROUPS * S5_STATE
PEER_HEADS, PEER_NKEYS, PEER_TOPK = 8, 128, 16
PEER_EXPERTS = PEER_NKEYS * PEER_NKEYS
PEER_HALF = 128
PEER_SLOTS = PEER_HEADS * PEER_TOPK

C_RQ, C_RK, C_RV, C_RG, C_GQKV, C_GG, C_SU = 0, 512, 1024, 2048, 3072, 6144, 7168
H_COLS = 9216
IN_A0 = 2 * RET_QK + 2 * RET_WIDTH + GDN_CONV_CH + GDN_WIDTH
AB_PAD = 128

S5_RUN = 64
S5_ROWS = 8 * S5_RUN
S5_LB = 512
S5_CB = S5_LB // S5_STATE * S5_GROUP_CH
S5_NCB = S5_LANES // S5_LB

VMEM_LIMIT = 56 * 1024 * 1024


def _cparams(sem):
    return pltpu.CompilerParams(dimension_semantics=sem, vmem_limit_bytes=VMEM_LIMIT)


def _dot(a, b, precision=None):
    return jnp.dot(a, b, preferred_element_type=F32, precision=precision)


def _dot_nt(a, b, precision=None):
    return lax.dot_general(a, b, (((1,), (1,)), ((), ())), preferred_element_type=F32,
                           precision=precision)


def _mm(x):
    return x.astype(MXU_DTYPE)


def _sigmoid(x):
    return 1.0 / (1.0 + jnp.exp(-x))


def _silu(x):
    return x * _sigmoid(x)


def _gelu_tanh(x):
    return 0.5 * x * (1.0 + jnp.tanh(math.sqrt(2.0 / math.pi) * (x + 0.044715 * (x * x * x))))


def _norm_mm_kernel(x_ref, g_ref, w_ref, we_ref, o_ref, xn_ref, oe_ref):
    @pl.when(pl.program_id(1) == 0)
    def _():
        x = x_ref[...]
        ms = jnp.mean(x * x, axis=-1, keepdims=True)
        xn = _mm(x * lax.rsqrt(ms + EPS) * g_ref[...])
        xn_ref[...] = xn
        oe_ref[...] = _dot(xn, we_ref[...])

    o_ref[...] = _dot(xn_ref[...], w_ref[...])


def _norm_mm(x, g, w, w_extra, tm=512, tn=1024):
    n, d = x.shape
    nc = w.shape[1]
    ne = w_extra.shape[1]
    return pl.pallas_call(
        _norm_mm_kernel,
        grid=(n // tm, nc // tn),
        in_specs=[pl.BlockSpec((tm, d), lambda i, j: (i, 0)),
                  pl.BlockSpec((1, d), lambda i, j: (0, 0)),
                  pl.BlockSpec((d, tn), lambda i, j: (0, j)),
                  pl.BlockSpec((d, ne), lambda i, j: (0, 0))],
        out_specs=[pl.BlockSpec((tm, tn), lambda i, j: (i, j)),
                   pl.BlockSpec((tm, d), lambda i, j: (i, 0)),
                   pl.BlockSpec((tm, ne), lambda i, j: (i, 0))],
        out_shape=[jax.ShapeDtypeStruct((n, nc), F32),
                   jax.ShapeDtypeStruct((n, d), MXU_DTYPE),
                   jax.ShapeDtypeStruct((n, ne), F32)],
        compiler_params=_cparams(("parallel", "arbitrary")),
        name="norm_proj",
    )(x, g.reshape(1, d), w, w_extra)


def _stream_of_chunk(c, n_prompt_chunks):
    return jnp.where(c < n_prompt_chunks, 0, c - n_prompt_chunks + 1)


def _ret_kernel(n_prompt_chunks, q_ref, k_ref, v_ref, gate_ref, cos_ref, sin_ref, s0_ref, ng_ref,
                o_ref, sout_ref, s_scr):
    c = pl.program_id(0)

    @pl.when(jnp.logical_or(c == 0, c >= n_prompt_chunks))
    def _():
        s_scr[...] = s0_ref[0]

    cos = cos_ref[...]
    sin = sin_ref[...]
    lane = lax.broadcasted_iota(jnp.int32, (CHUNK, RET_QK), 1)
    first_half = (lane % RET_DK) < (RET_DK // 2)

    def rope(x):
        swapped = jnp.where(first_half, pltpu.roll(x, RET_QK - RET_DK // 2, 1),
                            pltpu.roll(x, RET_DK // 2, 1))
        return x * cos + swapped * sin

    q = rope(q_ref[...])
    k = rope(k_ref[...]) * (RET_DK ** -0.5)
    v = v_ref[...]
    gate = gate_ref[...]
    ng = ng_ref[...]

    ti = lax.broadcasted_iota(jnp.int32, (CHUNK, CHUNK), 0)
    tj = lax.broadcasted_iota(jnp.int32, (CHUNK, CHUNK), 1)
    diff = (ti - tj).astype(F32)
    pos = lax.broadcasted_iota(jnp.int32, (CHUNK, 1), 0).astype(F32)

    heads = range(RET_HEADS)
    ld = [math.log(1.0 - 2.0 ** (-5.0 - h)) for h in heads]
    qm = [_mm(q[:, h * RET_DK:(h + 1) * RET_DK]) for h in heads]
    kh = [k[:, h * RET_DK:(h + 1) * RET_DK] for h in heads]
    vm = [_mm(v[:, h * RET_DV:(h + 1) * RET_DV]) for h in heads]
    s = [s_scr[h] for h in heads]
    scores = [_dot_nt(qm[h], _mm(kh[h])) * jnp.where(diff >= 0, jnp.exp(ld[h] * jnp.maximum(diff, 0.0)), 0.0)
              for h in heads]
    cross = [_dot(qm[h], _mm(s[h])) * jnp.exp(ld[h] * (pos + 1.0)) for h in heads]
    o = [_dot(_mm(scores[h]), vm[h]) + cross[h] for h in heads]
    for h in heads:
        kd = kh[h] * jnp.exp(ld[h] * (CHUNK - 1.0 - pos))
        s_scr[h] = s[h] * math.exp(ld[h] * CHUNK) + _dot(_mm(kd.T), vm[h])
    for h in heads:
        mu = jnp.mean(o[h], axis=-1, keepdims=True)
        oc = o[h] - mu
        var = jnp.mean(oc * oc, axis=-1, keepdims=True)
        oh = oc * lax.rsqrt(var + EPS) * ng[:, h * RET_DV:(h + 1) * RET_DV]
        oh = oh * _silu(gate[:, h * RET_DV:(h + 1) * RET_DV])
        o_ref[:, h * RET_DV:(h + 1) * RET_DV] = oh.astype(o_ref.dtype)

    sout_ref[0] = s_scr[...]


def _retention(h, cos, sin, s0, ret_norm, n_prompt_chunks):
    n = h.shape[0]
    nchunks = n // CHUNK
    nstreams = s0.shape[0]
    smap = lambda c: (_stream_of_chunk(c, n_prompt_chunks), 0, 0, 0)
    return pl.pallas_call(
        functools.partial(_ret_kernel, n_prompt_chunks),
        grid=(nchunks,),
        in_specs=[pl.BlockSpec((CHUNK, RET_QK), lambda c: (c, C_RQ // RET_QK)),
                  pl.BlockSpec((CHUNK, RET_QK), lambda c: (c, C_RK // RET_QK)),
                  pl.BlockSpec((CHUNK, RET_WIDTH), lambda c: (c, C_RV // RET_WIDTH)),
                  pl.BlockSpec((CHUNK, RET_WIDTH), lambda c: (c, C_RG // RET_WIDTH)),
                  pl.BlockSpec((CHUNK, RET_QK), lambda c: (c, 0)),
                  pl.BlockSpec((CHUNK, RET_QK), lambda c: (c, 0)),
                  pl.BlockSpec((1, RET_HEADS, RET_DK, RET_DV), smap),
                  pl.BlockSpec((1, RET_WIDTH), lambda c: (0, 0))],
        out_specs=[pl.BlockSpec((CHUNK, RET_WIDTH), lambda c: (c, 0)),
                   pl.BlockSpec((1, RET_HEADS, RET_DK, RET_DV), smap)],
        out_shape=[jax.ShapeDtypeStruct((n, RET_WIDTH), MXU_DTYPE),
                   jax.ShapeDtypeStruct((nstreams, RET_HEADS, RET_DK, RET_DV), F32)],
        scratch_shapes=[pltpu.VMEM((RET_HEADS, RET_DK, RET_DV), F32)],
        compiler_params=_cparams(("arbitrary",)),
        name="retention",
    )(h, h, h, h, cos, sin, s0, ret_norm.reshape(1, RET_WIDTH))


HIST_ROW0 = 8 - (CONV_W - 1)


def _split(x):
    hi = _mm(x)
    return hi, _mm(x - hi.astype(F32))


def _dot_split(a, b):
    return _dot(a[0], b[0]) + (_dot(a[0], b[1]) + _dot(a[1], b[0]))


def _unit_lower_inverses(mats):
    n = mats[0].shape[0]
    ti = lax.broadcasted_iota(jnp.int32, (n, n), 0)
    tj = lax.broadcasted_iota(jnp.int32, (n, n), 1)
    eye = (ti == tj).astype(F32)
    inv = [eye - a for a in mats]
    ps = [_split(a) for a in mats]
    k = 2
    while k < n:
        ps = [_split(_dot_split(x, x)) for x in ps]
        inv = [i + _dot_split(_split(i), x) for i, x in zip(inv, ps)]
        k *= 2
    return inv


def _gdn_kernel(n_prompt_chunks, qkv_ref, gate_ref, ab_ref, cw_ref, alog_ref, dtb_ref, ng_ref,
                hist0_ref, s0_ref, o_ref, hist_out_ref, sout_ref, xin_scr, s_scr):
    c = pl.program_id(0)

    @pl.when(jnp.logical_or(c == 0, c >= n_prompt_chunks))
    def _():
        s_scr[...] = s0_ref[0]
        xin_scr[HIST_ROW0:8, :] = hist0_ref[0]

    xin_scr[8:8 + CHUNK, :] = qkv_ref[...]
    cw = cw_ref[...]
    y = xin_scr[HIST_ROW0:HIST_ROW0 + CHUNK, :] * cw[0:1, :]
    for w in range(1, CONV_W):
        y = y + xin_scr[HIST_ROW0 + w:HIST_ROW0 + w + CHUNK, :] * cw[w:w + 1, :]
    y = _silu(y)
    new_hist = xin_scr[8 + CHUNK - (CONV_W - 1):8 + CHUNK, :]
    hist_out_ref[0] = new_hist
    xin_scr[HIST_ROW0:8, :] = new_hist

    ab = ab_ref[...]
    x = ab + dtb_ref[...]
    softplus = jnp.maximum(x, 0.0) + jnp.log1p(jnp.exp(-jnp.abs(x)))
    g_all = -jnp.exp(alog_ref[...]) * softplus
    beta_all = _sigmoid(ab)

    ti = lax.broadcasted_iota(jnp.int32, (CHUNK, CHUNK), 0)
    tj = lax.broadcasted_iota(jnp.int32, (CHUNK, CHUNK), 1)
    lower = ti >= tj
    strict = ti > tj
    gc_all = _dot(lower.astype(F32), g_all, HIGHEST)
    gc_t = gc_all.T
    gate = gate_ref[...]
    ng = ng_ref[...]

    heads = range(GDN_HEADS)
    gcol = [gc_all[:, h:h + 1] for h in heads]
    decay = [jnp.where(lower, jnp.exp(jnp.where(lower, gcol[h] - gc_t[h:h + 1, :], 0.0)), 0.0)
             for h in heads]
    beta = [beta_all[:, GDN_HEADS + h:GDN_HEADS + h + 1] for h in heads]
    egc = [jnp.exp(g) for g in gcol]
    g_last = [gc_all[CHUNK - 1:CHUNK, h:h + 1] for h in heads]

    def unit(xc):
        return xc * lax.rsqrt(jnp.sum(xc * xc, axis=-1, keepdims=True) + EPS)

    qh = [unit(y[:, h * GDN_DK:(h + 1) * GDN_DK]) * (GDN_DK ** -0.5) for h in heads]
    kh = [unit(y[:, GDN_WIDTH + h * GDN_DK:GDN_WIDTH + (h + 1) * GDN_DK]) for h in heads]
    vc = [y[:, 2 * GDN_WIDTH + h * GDN_DV:2 * GDN_WIDTH + (h + 1) * GDN_DV] for h in heads]
    kb = [kh[h] * beta[h] for h in heads]
    khm = [_mm(k) for k in kh]
    a_mat = [jnp.where(strict, _dot_nt(_mm(kb[h]), khm[h]) * decay[h], 0.0) for h in heads]
    attn = [_mm(_dot_nt(_mm(qh[h]), khm[h]) * decay[h]) for h in heads]
    t_inv = _unit_lower_inverses(a_mat)
    sol = [_dot_split(_split(t_inv[h]),
                      _split(jnp.concatenate([vc[h] * beta[h], kb[h] * egc[h]], axis=-1)))
           for h in heads]
    s = [s_scr[h] for h in heads]
    sm = [_mm(x) for x in s]
    v_new = [sol[h][:, :GDN_DV] - _dot(_mm(sol[h][:, GDN_DV:]), sm[h]) for h in heads]
    vm = [_mm(v) for v in v_new]
    o = [_dot(_mm(qh[h] * egc[h]), sm[h]) + _dot(attn[h], vm[h]) for h in heads]
    for h in heads:
        kd = kh[h] * jnp.exp(g_last[h] - gcol[h])
        s_scr[h] = s[h] * jnp.exp(g_last[h]) + _dot(_mm(kd.T), vm[h])
    for h in heads:
        oh = o[h] * lax.rsqrt(jnp.mean(o[h] * o[h], axis=-1, keepdims=True) + EPS) * ng
        oh = oh * _silu(gate[:, h * GDN_DV:(h + 1) * GDN_DV])
        o_ref[:, h * GDN_DV:(h + 1) * GDN_DV] = oh.astype(o_ref.dtype)

    sout_ref[0] = s_scr[...]


def _gdn(h, ab, conv_w, a_log, dt_bias, gdn_norm, hist0, s0, n_prompt_chunks):
    n = h.shape[0]
    nchunks = n // CHUNK
    nstreams = s0.shape[0]
    smap4 = lambda c: (_stream_of_chunk(c, n_prompt_chunks), 0, 0, 0)
    smap3 = lambda c: (_stream_of_chunk(c, n_prompt_chunks), 0, 0)
    alog_row = jnp.zeros((1, AB_PAD), F32).at[0, :GDN_HEADS].set(a_log)
    dtb_row = jnp.zeros((1, AB_PAD), F32).at[0, :GDN_HEADS].set(dt_bias)
    return pl.pallas_call(
        functools.partial(_gdn_kernel, n_prompt_chunks),
        grid=(nchunks,),
        in_specs=[pl.BlockSpec((CHUNK, GDN_CONV_CH), lambda c: (c, C_GQKV // GDN_CONV_CH)),
                  pl.BlockSpec((CHUNK, GDN_WIDTH), lambda c: (c, C_GG // GDN_WIDTH)),
                  pl.BlockSpec((CHUNK, AB_PAD), lambda c: (c, 0)),
                  pl.BlockSpec((CONV_W, GDN_CONV_CH), lambda c: (0, 0)),
                  pl.BlockSpec((1, AB_PAD), lambda c: (0, 0)),
                  pl.BlockSpec((1, AB_PAD), lambda c: (0, 0)),
                  pl.BlockSpec((1, GDN_DV), lambda c: (0, 0)),
                  pl.BlockSpec((1, CONV_W - 1, GDN_CONV_CH), smap3),
                  pl.BlockSpec((1, GDN_HEADS, GDN_DK, GDN_DV), smap4)],
        out_specs=[pl.BlockSpec((CHUNK, GDN_WIDTH), lambda c: (c, 0)),
                   pl.BlockSpec((1, CONV_W - 1, GDN_CONV_CH), smap3),
                   pl.BlockSpec((1, GDN_HEADS, GDN_DK, GDN_DV), smap4)],
        out_shape=[jax.ShapeDtypeStruct((n, GDN_WIDTH), MXU_DTYPE),
                   jax.ShapeDtypeStruct((nstreams, CONV_W - 1, GDN_CONV_CH), F32),
                   jax.ShapeDtypeStruct((nstreams, GDN_HEADS, GDN_DK, GDN_DV), F32)],
        scratch_shapes=[pltpu.VMEM((8 + CHUNK, GDN_CONV_CH), F32),
                        pltpu.VMEM((GDN_HEADS, GDN_DK, GDN_DV), F32)],
        compiler_params=_cparams(("arbitrary",)),
        name="gated_delta",
    )(h, h, ab, conv_w, alog_row, dtb_row, gdn_norm.reshape(1, GDN_DV), hist0, s0)


def _s5_prep_kernel(lre_ref, lim_ref, ldt_ref, btr_ref, bti_ref,
                    bbr_ref, bbi_ref, apr_ref, api_ref):
    lam_re = lre_ref[...]
    lam_im = lim_ref[...]
    dt = jnp.exp(ldt_ref[...])
    mag = jnp.exp(lam_re * dt)
    a_re = mag * jnp.cos(lam_im * dt)
    a_im = mag * jnp.sin(lam_im * dt)
    den = lam_re * lam_re + lam_im * lam_im
    f_re = ((a_re - 1.0) * lam_re + a_im * lam_im) / den
    f_im = (a_im * lam_re - (a_re - 1.0) * lam_im) / den
    btr = btr_ref[...]
    bti = bti_ref[...]
    bbr_ref[...] = f_re * btr - f_im * bti
    bbi_ref[...] = f_re * bti + f_im * btr
    p_re, p_im = a_re, a_im
    apr_ref[0] = p_re
    api_ref[0] = p_im
    for v in range(1, S5_RUN):
        p_re, p_im = p_re * a_re - p_im * a_im, p_re * a_im + p_im * a_re
        apr_ref[v] = p_re
        api_ref[v] = p_im


def _s5_prep(lam_re, lam_im, log_dt, b_re, b_im):
    g, p = lam_re.shape
    nch = b_re.shape[-1]
    return pl.pallas_call(
        _s5_prep_kernel,
        out_shape=[jax.ShapeDtypeStruct((g, nch, p), F32), jax.ShapeDtypeStruct((g, nch, p), F32),
                   jax.ShapeDtypeStruct((S5_RUN, g, 1, p), F32),
                   jax.ShapeDtypeStruct((S5_RUN, g, 1, p), F32)],
        name="s5_discretise",
    )(lam_re.reshape(g, 1, p), lam_im.reshape(g, 1, p), log_dt.reshape(g, 1, 1),
      jnp.swapaxes(b_re, 1, 2), jnp.swapaxes(b_im, 1, 2))


def _s5_kernel(n_prompt_blocks, runs_per_stream, u_ref, bbw_ref, cre_ref, cim_ref, apr_ref, api_ref,
               x0r_ref, x0i_ref, d_ref, z_ref, endr_ref, endi_ref, xr_scr, xi_scr, car_scr, cai_scr):
    b = pl.program_id(1)
    u_perm = jnp.concatenate([u_ref[pl.ds(v, 8, stride=S5_RUN), :] for v in range(S5_RUN)], axis=0)
    e = _dot(_mm(u_perm), bbw_ref[0])
    a_re = jnp.broadcast_to(apr_ref[0:1, :], (8, S5_LB))
    a_im = jnp.broadcast_to(api_ref[0:1, :], (8, S5_LB))

    xr = e[0:8, :S5_LB]
    xi = e[0:8, S5_LB:]
    xr_scr[0:8, :] = xr
    xi_scr[0:8, :] = xi
    for v in range(1, S5_RUN):
        er = e[8 * v:8 * v + 8, :S5_LB]
        ei = e[8 * v:8 * v + 8, S5_LB:]
        xr, xi = a_re * xr - a_im * xi + er, a_re * xi + a_im * xr + ei
        xr_scr[8 * v:8 * v + 8, :] = xr
        xi_scr[8 * v:8 * v + 8, :] = xi

    al_re = apr_ref[S5_RUN - 1:S5_RUN, :]
    al_im = api_ref[S5_RUN - 1:S5_RUN, :]
    x0r = x0r_ref[0]
    x0i = x0i_ref[0]
    is_sample = b >= n_prompt_blocks
    prev_r = car_scr[...]
    prev_i = cai_scr[...]
    cin_r, cin_i, end_r, end_i = [], [], [], []
    for r in range(8):
        if r % runs_per_stream == 0:
            reset = jnp.logical_or(is_sample, b == 0) if r == 0 else is_sample
        else:
            reset = None
        if reset is None:
            cr, ci = prev_r, prev_i
        else:
            cr = jnp.where(reset, x0r[r:r + 1, :], prev_r)
            ci = jnp.where(reset, x0i[r:r + 1, :], prev_i)
        prev_r = al_re * cr - al_im * ci + xr[r:r + 1, :]
        prev_i = al_re * ci + al_im * cr + xi[r:r + 1, :]
        cin_r.append(cr)
        cin_i.append(ci)
        end_r.append(prev_r)
        end_i.append(prev_i)
    car_scr[...] = prev_r
    cai_scr[...] = prev_i
    endr_ref[0] = jnp.concatenate(end_r, axis=0)
    endi_ref[0] = jnp.concatenate(end_i, axis=0)
    cr = jnp.concatenate(cin_r, axis=0)
    ci = jnp.concatenate(cin_i, axis=0)

    for v in range(S5_RUN):
        pr = jnp.broadcast_to(apr_ref[v:v + 1, :], (8, S5_LB))
        pi = jnp.broadcast_to(api_ref[v:v + 1, :], (8, S5_LB))
        xr_scr[8 * v:8 * v + 8, :] = xr_scr[8 * v:8 * v + 8, :] + (pr * cr - pi * ci)
        xi_scr[8 * v:8 * v + 8, :] = xi_scr[8 * v:8 * v + 8, :] + (pr * ci + pi * cr)

    y = _dot(_mm(xr_scr[...]), cre_ref[0]) - _dot(_mm(xi_scr[...]), cim_ref[0])
    z = _gelu_tanh(y + d_ref[...] * u_perm)
    for v in range(S5_RUN):
        z_ref[pl.ds(v, 8, stride=S5_RUN), :] = z[8 * v:8 * v + 8, :]


def _s5_scan(h, bbw, cre, cim, apow_re, apow_im, x0_re, x0_im, d_skip, n_prompt_blocks, runs_per_stream):
    n = h.shape[0]
    nblocks = n // S5_ROWS
    xmap = lambda j, b: (jnp.where(b < n_prompt_blocks, 0, b - n_prompt_blocks + 1), 0, j)
    return pl.pallas_call(
        functools.partial(_s5_kernel, n_prompt_blocks, runs_per_stream),
        grid=(S5_NCB, nblocks),
        in_specs=[pl.BlockSpec((S5_ROWS, S5_CB), lambda j, b: (b, C_SU // S5_CB + j)),
                  pl.BlockSpec((1, S5_CB, 2 * S5_LB), lambda j, b: (j, 0, 0)),
                  pl.BlockSpec((1, S5_LB, S5_CB), lambda j, b: (j, 0, 0)),
                  pl.BlockSpec((1, S5_LB, S5_CB), lambda j, b: (j, 0, 0)),
                  pl.BlockSpec((S5_RUN, S5_LB), lambda j, b: (0, j)),
                  pl.BlockSpec((S5_RUN, S5_LB), lambda j, b: (0, j)),
                  pl.BlockSpec((1, 8, S5_LB), xmap),
                  pl.BlockSpec((1, 8, S5_LB), xmap),
                  pl.BlockSpec((1, S5_CB), lambda j, b: (0, j))],
        out_specs=[pl.BlockSpec((S5_ROWS, S5_CB), lambda j, b: (b, j)),
                   pl.BlockSpec((1, 8, S5_LB), lambda j, b: (b, 0, j)),
                   pl.BlockSpec((1, 8, S5_LB), lambda j, b: (b, 0, j))],
        out_shape=[jax.ShapeDtypeStruct((n, S5_WIDTH), F32),
                   jax.ShapeDtypeStruct((nblocks, 8, S5_LANES), F32),
                   jax.ShapeDtypeStruct((nblocks, 8, S5_LANES), F32)],
        scratch_shapes=[pltpu.VMEM((S5_ROWS, S5_LB), F32), pltpu.VMEM((S5_ROWS, S5_LB), F32),
                        pltpu.VMEM((1, S5_LB), F32), pltpu.VMEM((1, S5_LB), F32)],
        compiler_params=_cparams(("parallel", "arbitrary")),
        name="s5_scan",
    )(h, bbw, cre, cim, apow_re, apow_im, x0_re, x0_im, d_skip.reshape(1, S5_WIDTH))


def _glu_kernel(zr_ref, w_ref, b_ref, zc_ref, o_ref):
    lin = _dot(_mm(zr_ref[...]), w_ref[...]) + b_ref[...]
    o_ref[...] = (zc_ref[...] * _sigmoid(lin)).astype(o_ref.dtype)


def _glu(z, w, bias, tm=512, tn=512):
    n, k = z.shape
    nc = w.shape[1]
    return pl.pallas_call(
        _glu_kernel,
        grid=(n // tm, nc // tn),
        in_specs=[pl.BlockSpec((tm, k), lambda i, j: (i, 0)),
                  pl.BlockSpec((k, tn), lambda i, j: (0, j)),
                  pl.BlockSpec((1, tn), lambda i, j: (0, j)),
                  pl.BlockSpec((tm, tn), lambda i, j: (i, j))],
        out_specs=pl.BlockSpec((tm, tn), lambda i, j: (i, j)),
        out_shape=jax.ShapeDtypeStruct((n, nc), MXU_DTYPE),
        compiler_params=_cparams(("parallel", "parallel")),
        name="s5_glu",
    )(z, w, bias.reshape(1, nc), z)


def _mix_out_kernel(x_ref, a_ref, b_ref, c_ref, w_ref, o_ref):
    ka = a_ref.shape[1]
    kb = b_ref.shape[1]
    acc = _dot(a_ref[...], w_ref[0:ka, :])
    acc = acc + _dot(b_ref[...], w_ref[ka:ka + kb, :])
    acc = acc + _dot(c_ref[...], w_ref[ka + kb:, :])
    o_ref[...] = x_ref[...] + acc


def _mix_out(x, o_ret, o_gdn, o_s5, w, tm=512, tn=1024):
    n, d = x.shape
    k = w.shape[0]
    return pl.pallas_call(
        _mix_out_kernel,
        grid=(n // tm, d // tn),
        in_specs=[pl.BlockSpec((tm, tn), lambda i, j: (i, j)),
                  pl.BlockSpec((tm, o_ret.shape[1]), lambda i, j: (i, 0)),
                  pl.BlockSpec((tm, o_gdn.shape[1]), lambda i, j: (i, 0)),
                  pl.BlockSpec((tm, o_s5.shape[1]), lambda i, j: (i, 0)),
                  pl.BlockSpec((k, tn), lambda i, j: (0, j))],
        out_specs=pl.BlockSpec((tm, tn), lambda i, j: (i, j)),
        out_shape=jax.ShapeDtypeStruct((n, d), F32),
        compiler_params=_cparams(("parallel", "parallel")),
        name="mix_out_proj",
    )(x, o_ret, o_gdn, o_s5, w)


def _peer_hidden_kernel(a_ref, w_ref, o_ref):
    o_ref[...] = _dot(a_ref[...], w_ref[...]).astype(o_ref.dtype)


def _peer_hidden(a, w, tm=512, tn=1024):
    n, k = a.shape
    return pl.pallas_call(
        _peer_hidden_kernel,
        grid=(n // tm, PEER_EXPERTS // tn),
        in_specs=[pl.BlockSpec((tm, k), lambda i, j: (i, 0)),
                  pl.BlockSpec((k, tn), lambda i, j: (0, j))],
        out_specs=pl.BlockSpec((tm, tn), lambda i, j: (i, j)),
        out_shape=jax.ShapeDtypeStruct((n, PEER_EXPERTS), MXU_DTYPE),
        compiler_params=_cparams(("parallel", "parallel")),
        name="peer_hidden_dense",
    )(a, w)


def _peer_out_kernel(x_ref, c_ref, w_ref, o_ref):
    @pl.when(pl.program_id(2) == 0)
    def _():
        o_ref[...] = x_ref[...]

    o_ref[...] += _dot(c_ref[...], w_ref[...])


def _peer_out(x, c, w, tm=512, tn=1024, tk=4096):
    n, d = x.shape
    return pl.pallas_call(
        _peer_out_kernel,
        grid=(n // tm, d // tn, PEER_EXPERTS // tk),
        in_specs=[pl.BlockSpec((tm, tn), lambda i, j, kk: (i, j)),
                  pl.BlockSpec((tm, tk), lambda i, j, kk: (i, kk)),
                  pl.BlockSpec((tk, tn), lambda i, j, kk: (kk, j))],
        out_specs=pl.BlockSpec((tm, tn), lambda i, j, kk: (i, j)),
        out_shape=jax.ShapeDtypeStruct((n, d), F32),
        compiler_params=_cparams(("parallel", "parallel", "arbitrary")),
        name="peer_out_dense",
    )(x, c, w)


def _final_norm_kernel(n_prompt_tiles, x_ref, g_ref, op_ref, os_ref):
    x = x_ref[...]
    ms = jnp.mean(x * x, axis=-1, keepdims=True)
    y = x * lax.rsqrt(ms + EPS) * g_ref[...]
    i = pl.program_id(0)

    @pl.when(i < n_prompt_tiles)
    def _():
        op_ref[...] = y

    @pl.when(i >= n_prompt_tiles)
    def _():
        os_ref[...] = y


def _final_norm(x, g, n_prompt, tm=512):
    n, d = x.shape
    npt = n_prompt // tm
    return pl.pallas_call(
        functools.partial(_final_norm_kernel, npt),
        grid=(n // tm,),
        in_specs=[pl.BlockSpec((tm, d), lambda i: (i, 0)), pl.BlockSpec((1, d), lambda i: (0, 0))],
        out_specs=[pl.BlockSpec((tm, d), lambda i: (jnp.minimum(i, npt - 1), 0)),
                   pl.BlockSpec((tm, d), lambda i: (jnp.maximum(i - npt, 0), 0))],
        out_shape=[jax.ShapeDtypeStruct((n_prompt, d), F32),
                   jax.ShapeDtypeStruct((n - n_prompt, d), F32)],
        compiler_params=_cparams(("arbitrary",)),
        name="final_norm",
    )(x, g.reshape(1, d))


NEG_INF = float("-inf")


def _top16_rows(s, ids, val_ref, id_ref):
    big = float(2 ** 20)
    for r in range(PEER_TOPK):
        m = jnp.max(s, axis=0, keepdims=True)
        i = jnp.min(jnp.where(s == m, ids, big), axis=0, keepdims=True)
        s = jnp.where(ids == i, NEG_INF, s)
        val_ref[r:r + 1, :] = m
        id_ref[r:r + 1, :] = i
    return val_ref[...], id_ref[...]


def _peer_topk_kernel(q_ref, keys_ref, e1_ref, e2_ref, gate_ref,
                      v0_scr, i0_scr, v1_scr, i1_scr, vb_scr, ib_scr, e1_scr, e2_scr, g_scr):
    tt = q_ref.shape[0]
    key_ids = lax.broadcasted_iota(jnp.int32, (PEER_NKEYS, tt), 0).astype(F32)
    row16 = lax.broadcasted_iota(jnp.int32, (PEER_TOPK, tt), 0).astype(F32)
    row8 = lax.broadcasted_iota(jnp.int32, (8, tt), 0).astype(F32)
    cand_ids = [row16]
    cand_ok = [row16 >= 0.0]
    for i in range(1, 8):
        cand_ids.append(row8 + 16.0 * i)
        cand_ok.append(row8 < float(PEER_TOPK // (i + 1)))
    cand_ids.append(16.0 * (row8 + 8.0))
    cand_ok.append(row8 >= 0.0)
    cand_ids = jnp.concatenate(cand_ids, axis=0)
    cand_ok = jnp.concatenate(cand_ok, axis=0)

    for h in range(PEER_HEADS):
        col = 2 * h * PEER_HALF
        s0 = _dot_nt(keys_ref[2 * h], _mm(q_ref[:, col:col + PEER_HALF]))
        t0, ti0 = _top16_rows(s0, key_ids, v0_scr, i0_scr)
        s1 = _dot_nt(keys_ref[2 * h + 1], _mm(q_ref[:, col + PEER_HALF:col + 2 * PEER_HALF]))
        t1, ti1 = _top16_rows(s1, key_ids, v1_scr, i1_scr)
        cand = [t0[0:1] + t1]
        for i in range(1, 8):
            cand.append(t0[i:i + 1] + t1[0:8])
        cand.append(t0[8:16] + t1[0:1])
        cand = jnp.where(cand_ok, jnp.concatenate(cand, axis=0), NEG_INF)
        best, flat = _top16_rows(cand, cand_ids, vb_scr, ib_scr)
        flat = flat.astype(jnp.int32)
        i_sel = flat >> 4
        j_sel = flat & (PEER_TOPK - 1)
        e1 = jnp.zeros_like(best)
        e2 = jnp.zeros_like(best)
        for r in range(PEER_TOPK):
            e1 = jnp.where(i_sel == r, ti0[r:r + 1], e1)
            e2 = jnp.where(j_sel == r, ti1[r:r + 1], e2)
        p = jnp.exp(best - best[0:1])
        gate = p / jnp.sum(p, axis=0, keepdims=True)
        e1_scr[h * PEER_TOPK:(h + 1) * PEER_TOPK, :] = e1
        e2_scr[h * PEER_TOPK:(h + 1) * PEER_TOPK, :] = e2
        g_scr[h * PEER_TOPK:(h + 1) * PEER_TOPK, :] = gate
    e1_ref[...] = e1_scr[...].T
    e2_ref[...] = e2_scr[...].T
    gate_ref[...] = g_scr[...].T


def _peer_topk(q, keys, tt=256):
    n = q.shape[0]
    spec = pl.BlockSpec((tt, PEER_SLOTS), lambda i: (i, 0))
    rows16 = pltpu.VMEM((PEER_TOPK, tt), F32)
    slots = pltpu.VMEM((PEER_SLOTS, tt), F32)
    return pl.pallas_call(
        _peer_topk_kernel,
        grid=(n // tt,),
        in_specs=[pl.BlockSpec((tt, q.shape[1]), lambda i: (i, 0)),
                  pl.BlockSpec(keys.shape, lambda i: (0, 0, 0))],
        out_specs=[spec, spec, spec],
        out_shape=[jax.ShapeDtypeStruct((n, PEER_SLOTS), F32)] * 3,
        scratch_shapes=[rows16] * 6 + [slots] * 3,
        compiler_params=_cparams(("parallel",)),
        name="peer_topk",
    )(q, keys)


SEL_GROUP = 16
SEL_INTERLEAVE = 8


def _peer_select_kernel(h_ref, e1_ref, e2_ref, gate_ref, c_ref):
    tt = h_ref.shape[0]
    sub = lax.broadcasted_iota(jnp.int32, (PEER_NKEYS, PEER_SLOTS), 0).astype(F32)

    def body(g, carry):
        r0 = pl.multiple_of(g * SEL_GROUP, SEL_GROUP)
        x = h_ref[pl.ds(r0, SEL_GROUP), :]
        x3 = jnp.stack([x[:, e * PEER_NKEYS:(e + 1) * PEER_NKEYS] for e in range(PEER_NKEYS)], axis=0)
        h3 = jnp.swapaxes(x3, 0, 1)
        tiles = []
        for t0 in range(0, SEL_GROUP, SEL_INTERLEAVE):
            toks = range(t0, t0 + SEL_INTERLEAVE)
            on1 = [sub == e1_ref[pl.ds(r0 + t, 1), :] for t in toks]
            b2 = [jnp.where(sub == e2_ref[pl.ds(r0 + t, 1), :], 1.0, 0.0).astype(MXU_DTYPE)
                  for t in toks]
            m = [_dot(h3[t], b) for t, b in zip(toks, b2)]
            hid = [jnp.sum(jnp.where(o, v, 0.0), axis=0, keepdims=True) for o, v in zip(on1, m)]
            coef = [_gelu_tanh(v) * gate_ref[pl.ds(r0 + t, 1), :] for t, v in zip(toks, hid)]
            a1 = [jnp.where(o, c, 0.0).astype(MXU_DTYPE) for o, c in zip(on1, coef)]
            tiles += [_dot_nt(a, b).astype(c_ref.dtype) for a, b in zip(a1, b2)]
        y3 = jnp.swapaxes(jnp.stack(tiles, axis=0), 0, 1)
        for e in range(PEER_NKEYS):
            c_ref[pl.ds(r0, SEL_GROUP), e * PEER_NKEYS:(e + 1) * PEER_NKEYS] = y3[e]
        return carry

    lax.fori_loop(0, tt // SEL_GROUP, body, 0)


def _peer_select(h, e1, e2, gate, tt=64):
    n = h.shape[0]
    spec = pl.BlockSpec((tt, PEER_SLOTS), lambda i: (i, 0))
    dense = pl.BlockSpec((tt, PEER_EXPERTS), lambda i: (i, 0))
    return pl.pallas_call(
        _peer_select_kernel,
        grid=(n // tt,),
        in_specs=[dense, spec, spec, spec],
        out_specs=dense,
        out_shape=jax.ShapeDtypeStruct((n, PEER_EXPERTS), MXU_DTYPE),
        compiler_params=_cparams(("parallel",)),
        name="peer_select_scatter",
    )(h, e1, e2, gate)


def _rope_tables(pos):
    half = RET_DK // 2
    inv = ROPE_BASE ** (-jnp.arange(half, dtype=F32) / half)
    ang = pos.astype(F32)[:, None] * inv[None, :]
    cos = jnp.cos(ang)
    sin = jnp.sin(ang)
    cos_h = jnp.concatenate([cos, cos], axis=-1)
    sin_h = jnp.concatenate([-sin, sin], axis=-1)
    return jnp.tile(cos_h, (1, RET_HEADS)), jnp.tile(sin_h, (1, RET_HEADS))


def _with_zero_stream(state):
    return jnp.concatenate([jnp.zeros_like(state[:1]), state], axis=0)


def _block_diag(x):
    j, g, a, b = x.shape
    eye = jnp.eye(g, dtype=x.dtype)
    return (x[:, :, :, None, :] * eye[None, :, None, :, None]).reshape(j, g * a, g * b)


def kernel(x_prompt, x_sample, state_ret, state_gdn, cache_gdn_conv, state_s5_re, state_s5_im, norm_mix, w_in, w_out, ret_norm, gdn_conv_w, gdn_a_log, gdn_dt_bias, gdn_norm, s5_lam_re, s5_lam_im, s5_log_dt, s5_b_re, s5_b_im, s5_c_re, s5_c_im, s5_d, s5_glu_w, s5_glu_b, norm_ffn, peer_wq, peer_keys, peer_u, peer_v, norm_final):
    bp, tp, d = x_prompt.shape
    nb, ts, _ = x_sample.shape
    depth = w_in.shape[0]
    assert bp == 1 and d == D_MODEL and ts == CHUNK and tp % S5_ROWS == 0
    assert (nb * ts) % S5_ROWS == 0 and CHUNK % S5_RUN == 0
    n_prompt = tp
    n = tp + nb * ts
    n_prompt_chunks = n_prompt // CHUNK
    n_prompt_blocks = n_prompt // S5_ROWS
    runs_per_stream = ts // S5_RUN
    streams_per_block = S5_ROWS // ts

    x = jnp.concatenate([x_prompt.reshape(tp, d), x_sample.reshape(nb * ts, d)], axis=0)
    past_len = 2048
    pos = jnp.concatenate([jnp.arange(tp), jnp.tile(past_len + jnp.arange(ts), nb)])
    cos, sin = _rope_tables(pos)

    ret_out, gdn_out, conv_out, s5r_out, s5i_out = [], [], [], [], []
    for l in range(depth):
        w_l = w_in[l]
        w_main = _mm(jnp.concatenate([w_l[:, :IN_A0], w_l[:, IN_A0 + 2 * GDN_HEADS:]], axis=1))
        w_ab = _mm(jnp.pad(w_l[:, IN_A0:IN_A0 + 2 * GDN_HEADS], ((0, 0), (0, AB_PAD - 2 * GDN_HEADS))))

        h, _, ab = _norm_mm(x, norm_mix[l], w_main, w_ab)

        o_ret, s_ret = _retention(h, cos, sin, _with_zero_stream(state_ret[l]), ret_norm[l], n_prompt_chunks)
        o_gdn, s_conv, s_gdn = _gdn(h, ab, gdn_conv_w[l], gdn_a_log[l], gdn_dt_bias[l], gdn_norm[l],
                                    _with_zero_stream(cache_gdn_conv[l]), _with_zero_stream(state_gdn[l]),
                                    n_prompt_chunks)

        bbt_re, bbt_im, apow_re, apow_im = _s5_prep(s5_lam_re[l], s5_lam_im[l], s5_log_dt[l],
                                                    s5_b_re[l], s5_b_im[l])
        g4 = (S5_NCB, S5_GROUPS // S5_NCB)
        bbw = jnp.concatenate([_block_diag(bbt_re.reshape(*g4, S5_GROUP_CH, S5_STATE)),
                               _block_diag(bbt_im.reshape(*g4, S5_GROUP_CH, S5_STATE))], axis=-1)
        cre = _block_diag(jnp.swapaxes(s5_c_re[l], 1, 2).reshape(*g4, S5_STATE, S5_GROUP_CH))
        cim = _block_diag(jnp.swapaxes(s5_c_im[l], 1, 2).reshape(*g4, S5_STATE, S5_GROUP_CH))

        def run_states(s):
            s = s.reshape(nb // streams_per_block, streams_per_block, 1, S5_LANES)
            s = jnp.broadcast_to(s, (nb // streams_per_block, streams_per_block, runs_per_stream, S5_LANES))
            s = s.reshape(nb // streams_per_block, 8, S5_LANES)
            return jnp.concatenate([jnp.zeros_like(s[:1]), s], axis=0)

        z, end_re, end_im = _s5_scan(h, _mm(bbw), _mm(cre), _mm(cim),
                                     apow_re.reshape(S5_RUN, S5_LANES), apow_im.reshape(S5_RUN, S5_LANES),
                                     run_states(state_s5_re[l]), run_states(state_s5_im[l]), s5_d[l],
                                     n_prompt_blocks, runs_per_stream)
        o_s5 = _glu(z, _mm(s5_glu_w[l]), s5_glu_b[l])

        x = _mix_out(x, o_ret, o_gdn, o_s5, _mm(w_out[l]))

        q, xn, _ = _norm_mm(x, norm_ffn[l], _mm(peer_wq[l]), jnp.zeros((d, AB_PAD), MXU_DTYPE))
        e1, e2, gate = _peer_topk(q, _mm(peer_keys[l].reshape(2 * PEER_HEADS, PEER_NKEYS, PEER_HALF)))
        h3 = _peer_hidden(xn, _mm(peer_u[l].T))
        c3 = _peer_select(h3, e1, e2, gate)
        x = _peer_out(x, c3, _mm(peer_v[l]))

        def s5_states(e):
            p = e[n_prompt_blocks - 1, 7].reshape(1, S5_GROUPS, S5_STATE)
            s = e[n_prompt_blocks:].reshape(nb, runs_per_stream, S5_LANES)[:, -1]
            return p, s.reshape(nb, S5_GROUPS, S5_STATE)

        ret_out.append(s_ret)
        gdn_out.append(s_gdn)
        conv_out.append(s_conv)
        s5r_out.append(s5_states(end_re))
        s5i_out.append(s5_states(end_im))

    y_prompt, y_sample = _final_norm(x, norm_final, n_prompt)
    y_prompt = y_prompt.reshape(bp, tp, d)
    y_sample = y_sample.reshape(nb, ts, d)
    split = lambda outs: (jnp.stack([o[:1] for o in outs]), jnp.stack([o[1:] for o in outs]))
    p_ret, s_ret = split(ret_out)
    p_gdn, s_gdn = split(gdn_out)
    p_conv, s_conv = split(conv_out)
    p_s5_re, s_s5_re = (jnp.stack([o[0] for o in s5r_out]), jnp.stack([o[1] for o in s5r_out]))
    p_s5_im, s_s5_im = (jnp.stack([o[0] for o in s5i_out]), jnp.stack([o[1] for o in s5i_out]))
    return (y_prompt, y_sample, p_ret, p_gdn, p_conv, p_s5_re, p_s5_im,
            s_ret, s_gdn, s_conv, s_s5_re, s_s5_im)
```

```python
import functools
import math

import numpy as np
import jax
import jax.numpy as jnp
from jax import lax
from jax.experimental import pallas as pl
from jax.experimental.pallas import tpu as pltpu

F32 = jnp.float32
BF16 = jnp.bfloat16
MXU_DTYPE = BF16
HIGHEST = lax.Precision.HIGHEST

EPS = 1e-6
CHUNK = 64
D_MODEL = 4096
RET_HEADS, RET_DK, RET_DV = 8, 64, 128
RET_QK = RET_HEADS * RET_DK
RET_WIDTH = RET_HEADS * RET_DV
ROPE_BASE = 10000.0
GDN_HEADS, GDN_DK, GDN_DV = 8, 128, 128
GDN_WIDTH = GDN_HEADS * GDN_DV
CONV_W = 4
GDN_CONV_CH = GDN_HEADS * (2 * GDN_DK + GDN_DV)
S5_GROUP_CH, S5_GROUPS, S5_STATE = 16, 128, 64
S5_WIDTH = S5_GROUP_CH * S5_GROUPS
S5_LANES = S5_GROUPS * S5_STATE
PEER_HEADS, PEER_NKEYS, PEER_TOPK = 8, 128, 16
PEER_EXPERTS = PEER_NKEYS * PEER_NKEYS
PEER_HALF = 128
PEER_SLOTS = PEER_HEADS * PEER_TOPK

C_RQ, C_RK, C_RV, C_RG, C_GQKV, C_GG, C_SU = 0, 512, 1024, 2048, 3072, 6144, 7168
H_COLS = 9216
IN_A0 = 2 * RET_QK + 2 * RET_WIDTH + GDN_CONV_CH + GDN_WIDTH
AB_PAD = 128

S5_RUN = 64
S5_ROWS = 8 * S5_RUN
S5_LB = 512
S5_CB = S5_LB // S5_STATE * S5_GROUP_CH
S5_NCB = S5_LANES // S5_LB

VMEM_LIMIT = 56 * 1024 * 1024


def _cparams(sem):
    return pltpu.CompilerParams(dimension_semantics=sem, vmem_limit_bytes=VMEM_LIMIT)


def _dot(a, b, precision=None):
    return jnp.dot(a, b, preferred_element_type=F32, precision=precision)


def _dot_nt(a, b, precision=None):
    return lax.dot_general(a, b, (((1,), (1,)), ((), ())), preferred_element_type=F32,
                           precision=precision)


def _mm(x):
    return x.astype(MXU_DTYPE)


def _sigmoid(x):
    return 1.0 / (1.0 + jnp.exp(-x))


def _silu(x):
    return x * _sigmoid(x)


def _gelu_tanh(x):
    return 0.5 * x * (1.0 + jnp.tanh(math.sqrt(2.0 / math.pi) * (x + 0.044715 * (x * x * x))))


def _norm_mm_kernel(x_ref, g_ref, w_ref, we_ref, o_ref, xn_ref, oe_ref):
    @pl.when(pl.program_id(1) == 0)
    def _():
        x = x_ref[...]
        ms = jnp.mean(x * x, axis=-1, keepdims=True)
        xn = _mm(x * lax.rsqrt(ms + EPS) * g_ref[...])
        xn_ref[...] = xn
        oe_ref[...] = _dot(xn, we_ref[...])

    o_ref[...] = _dot(xn_ref[...], w_ref[...])


def _norm_mm(x, g, w, w_extra, tm=512, tn=1024):
    n, d = x.shape
    nc = w.shape[1]
    ne = w_extra.shape[1]
    return pl.pallas_call(
        _norm_mm_kernel,
        grid=(n // tm, nc // tn),
        in_specs=[pl.BlockSpec((tm, d), lambda i, j: (i, 0)),
                  pl.BlockSpec((1, d), lambda i, j: (0, 0)),
                  pl.BlockSpec((d, tn), lambda i, j: (0, j)),
                  pl.BlockSpec((d, ne), lambda i, j: (0, 0))],
        out_specs=[pl.BlockSpec((tm, tn), lambda i, j: (i, j)),
                   pl.BlockSpec((tm, d), lambda i, j: (i, 0)),
                   pl.BlockSpec((tm, ne), lambda i, j: (i, 0))],
        out_shape=[jax.ShapeDtypeStruct((n, nc), F32),
                   jax.ShapeDtypeStruct((n, d), MXU_DTYPE),
                   jax.ShapeDtypeStruct((n, ne), F32)],
        compiler_params=_cparams(("parallel", "arbitrary")),
        name="norm_proj",
    )(x, g.reshape(1, d), w, w_extra)


def _stream_of_chunk(c, n_prompt_chunks):
    return jnp.where(c < n_prompt_chunks, 0, c - n_prompt_chunks + 1)


def _ret_kernel(n_prompt_chunks, q_ref, k_ref, v_ref, gate_ref, cos_ref, sin_ref, s0_ref, ng_ref,
                o_ref, sout_ref, s_scr):
    c = pl.program_id(0)

    @pl.when(jnp.logical_or(c == 0, c >= n_prompt_chunks))
    def _():
        s_scr[...] = s0_ref[0]

    cos = cos_ref[...]
    sin = sin_ref[...]
    lane = lax.broadcasted_iota(jnp.int32, (CHUNK, RET_QK), 1)
    first_half = (lane % RET_DK) < (RET_DK // 2)

    def rope(x):
        swapped = jnp.where(first_half, pltpu.roll(x, RET_QK - RET_DK // 2, 1),
                            pltpu.roll(x, RET_DK // 2, 1))
        return x * cos + swapped * sin

    q = rope(q_ref[...])
    k = rope(k_ref[...]) * (RET_DK ** -0.5)
    v = v_ref[...]
    gate = gate_ref[...]
    ng = ng_ref[...]

    ti = lax.broadcasted_iota(jnp.int32, (CHUNK, CHUNK), 0)
    tj = lax.broadcasted_iota(jnp.int32, (CHUNK, CHUNK), 1)
    diff = (ti - tj).astype(F32)
    pos = lax.broadcasted_iota(jnp.int32, (CHUNK, 1), 0).astype(F32)

    heads = range(RET_HEADS)
    ld = [math.log(1.0 - 2.0 ** (-5.0 - h)) for h in heads]
    qm = [_mm(q[:, h * RET_DK:(h + 1) * RET_DK]) for h in heads]
    kh = [k[:, h * RET_DK:(h + 1) * RET_DK] for h in heads]
    vm = [_mm(v[:, h * RET_DV:(h + 1) * RET_DV]) for h in heads]
    s = [s_scr[h] for h in heads]
    scores = [_dot_nt(qm[h], _mm(kh[h])) * jnp.where(diff >= 0, jnp.exp(ld[h] * jnp.maximum(diff, 0.0)), 0.0)
              for h in heads]
    cross = [_dot(qm[h], _mm(s[h])) * jnp.exp(ld[h] * (pos + 1.0)) for h in heads]
    o = [_dot(_mm(scores[h]), vm[h]) + cross[h] for h in heads]
    for h in heads:
        kd = kh[h] * jnp.exp(ld[h] * (CHUNK - 1.0 - pos))
        s_scr[h] = s[h] * math.exp(ld[h] * CHUNK) + _dot(_mm(kd.T), vm[h])
    for h in heads:
        mu = jnp.mean(o[h], axis=-1, keepdims=True)
        oc = o[h] - mu
        var = jnp.mean(oc * oc, axis=-1, keepdims=True)
        oh = oc * lax.rsqrt(var + EPS) * ng[:, h * RET_DV:(h + 1) * RET_DV]
        oh = oh * _silu(gate[:, h * RET_DV:(h + 1) * RET_DV])
        o_ref[:, h * RET_DV:(h + 1) * RET_DV] = oh.astype(o_ref.dtype)

    sout_ref[0] = s_scr[...]


def _retention(h, cos, sin, s0, ret_norm, n_prompt_chunks):
    n = h.shape[0]
    nchunks = n // CHUNK
    nstreams = s0.shape[0]
    smap = lambda c: (_stream_of_chunk(c, n_prompt_chunks), 0, 0, 0)
    return pl.pallas_call(
        functools.partial(_ret_kernel, n_prompt_chunks),
        grid=(nchunks,),
        in_specs=[pl.BlockSpec((CHUNK, RET_QK), lambda c: (c, C_RQ // RET_QK)),
                  pl.BlockSpec((CHUNK, RET_QK), lambda c: (c, C_RK // RET_QK)),
                  pl.BlockSpec((CHUNK, RET_WIDTH), lambda c: (c, C_RV // RET_WIDTH)),
                  pl.BlockSpec((CHUNK, RET_WIDTH), lambda c: (c, C_RG // RET_WIDTH)),
                  pl.BlockSpec((CHUNK, RET_QK), lambda c: (c, 0)),
                  pl.BlockSpec((CHUNK, RET_QK), lambda c: (c, 0)),
                  pl.BlockSpec((1, RET_HEADS, RET_DK, RET_DV), smap),
                  pl.BlockSpec((1, RET_WIDTH), lambda c: (0, 0))],
        out_specs=[pl.BlockSpec((CHUNK, RET_WIDTH), lambda c: (c, 0)),
                   pl.BlockSpec((1, RET_HEADS, RET_DK, RET_DV), smap)],
        out_shape=[jax.ShapeDtypeStruct((n, RET_WIDTH), MXU_DTYPE),
                   jax.ShapeDtypeStruct((nstreams, RET_HEADS, RET_DK, RET_DV), F32)],
        scratch_shapes=[pltpu.VMEM((RET_HEADS, RET_DK, RET_DV), F32)],
        compiler_params=_cparams(("arbitrary",)),
        name="retention",
    )(h, h, h, h, cos, sin, s0, ret_norm.reshape(1, RET_WIDTH))


HIST_ROW0 = 8 - (CONV_W - 1)


def _split(x):
    hi = _mm(x)
    return hi, _mm(x - hi.astype(F32))


def _dot_split(a, b):
    return _dot(a[0], b[0]) + (_dot(a[0], b[1]) + _dot(a[1], b[0]))


def _unit_lower_inverses(mats):
    n = mats[0].shape[0]
    ti = lax.broadcasted_iota(jnp.int32, (n, n), 0)
    tj = lax.broadcasted_iota(jnp.int32, (n, n), 1)
    eye = (ti == tj).astype(F32)
    inv = [eye - a for a in mats]
    ps = [_split(a) for a in mats]
    k = 2
    while k < n:
        ps = [_split(_dot_split(x, x)) for x in ps]
        inv = [i + _dot_split(_split(i), x) for i, x in zip(inv, ps)]
        k *= 2
    return inv


def _gdn_kernel(n_prompt_chunks, qkv_ref, gate_ref, ab_ref, cw_ref, alog_ref, dtb_ref, ng_ref,
                hist0_ref, s0_ref, o_ref, hist_out_ref, sout_ref, xin_scr, s_scr):
    c = pl.program_id(0)

    @pl.when(jnp.logical_or(c == 0, c >= n_prompt_chunks))
    def _():
        s_scr[...] = s0_ref[0]
        xin_scr[HIST_ROW0:8, :] = hist0_ref[0]

    xin_scr[8:8 + CHUNK, :] = qkv_ref[...]
    cw = cw_ref[...]
    y = xin_scr[HIST_ROW0:HIST_ROW0 + CHUNK, :] * cw[0:1, :]
    for w in range(1, CONV_W):
        y = y + xin_scr[HIST_ROW0 + w:HIST_ROW0 + w + CHUNK, :] * cw[w:w + 1, :]
    y = _silu(y)
    new_hist = xin_scr[8 + CHUNK - (CONV_W - 1):8 + CHUNK, :]
    hist_out_ref[0] = new_hist
    xin_scr[HIST_ROW0:8, :] = new_hist

    ab = ab_ref[...]
    x = ab + dtb_ref[...]
    softplus = jnp.maximum(x, 0.0) + jnp.log1p(jnp.exp(-jnp.abs(x)))
    g_all = -jnp.exp(alog_ref[...]) * softplus
    beta_all = _sigmoid(ab)

    ti = lax.broadcasted_iota(jnp.int32, (CHUNK, CHUNK), 0)
    tj = lax.broadcasted_iota(jnp.int32, (CHUNK, CHUNK), 1)
    lower = ti >= tj
    strict = ti > tj
    gc_all = _dot(lower.astype(F32), g_all, HIGHEST)
    gc_t = gc_all.T
    gate = gate_ref[...]
    ng = ng_ref[...]

    heads = range(GDN_HEADS)
    gcol = [gc_all[:, h:h + 1] for h in heads]
    decay = [jnp.where(lower, jnp.exp(jnp.where(lower, gcol[h] - gc_t[h:h + 1, :], 0.0)), 0.0)
             for h in heads]
    beta = [beta_all[:, GDN_HEADS + h:GDN_HEADS + h + 1] for h in heads]
    egc = [jnp.exp(g) for g in gcol]
    g_last = [gc_all[CHUNK - 1:CHUNK, h:h + 1] for h in heads]

    def unit(xc):
        return xc * lax.rsqrt(jnp.sum(xc * xc, axis=-1, keepdims=True) + EPS)

    qh = [unit(y[:, h * GDN_DK:(h + 1) * GDN_DK]) * (GDN_DK ** -0.5) for h in heads]
    kh = [unit(y[:, GDN_WIDTH + h * GDN_DK:GDN_WIDTH + (h + 1) * GDN_DK]) for h in heads]
    vc = [y[:, 2 * GDN_WIDTH + h * GDN_DV:2 * GDN_WIDTH + (h + 1) * GDN_DV] for h in heads]
    kb = [kh[h] * beta[h] for h in heads]
    khm = [_mm(k) for k in kh]
    a_mat = [jnp.where(strict, _dot_nt(_mm(kb[h]), khm[h]) * decay[h], 0.0) for h in heads]
    attn = [_mm(_dot_nt(_mm(qh[h]), khm[h]) * decay[h]) for h in heads]
    t_inv = _unit_lower_inverses(a_mat)
    sol = [_dot_split(_split(t_inv[h]),
                      _split(jnp.concatenate([vc[h] * beta[h], kb[h] * egc[h]], axis=-1)))
           for h in heads]
    s = [s_scr[h] for h in heads]
    sm = [_mm(x) for x in s]
    v_new = [sol[h][:, :GDN_DV] - _dot(_mm(sol[h][:, GDN_DV:]), sm[h]) for h in heads]
    vm = [_mm(v) for v in v_new]
    o = [_dot(_mm(qh[h] * egc[h]), sm[h]) + _dot(attn[h], vm[h]) for h in heads]
    for h in heads:
        kd = kh[h] * jnp.exp(g_last[h] - gcol[h])
        s_scr[h] = s[h] * jnp.exp(g_last[h]) + _dot(_mm(kd.T), vm[h])
    for h in heads:
        oh = o[h] * lax.rsqrt(jnp.mean(o[h] * o[h], axis=-1, keepdims=True) + EPS) * ng
        oh = oh * _silu(gate[:, h * GDN_DV:(h + 1) * GDN_DV])
        o_ref[:, h * GDN_DV:(h + 1) * GDN_DV] = oh.astype(o_ref.dtype)

    sout_ref[0] = s_scr[...]


def _gdn(h, ab, conv_w, a_log, dt_bias, gdn_norm, hist0, s0, n_prompt_chunks):
    n = h.shape[0]
    nchunks = n // CHUNK
    nstreams = s0.shape[0]
    smap4 = lambda c: (_stream_of_chunk(c, n_prompt_chunks), 0, 0, 0)
    smap3 = lambda c: (_stream_of_chunk(c, n_prompt_chunks), 0, 0)
    alog_row = jnp.zeros((1, AB_PAD), F32).at[0, :GDN_HEADS].set(a_log)
    dtb_row = jnp.zeros((1, AB_PAD), F32).at[0, :GDN_HEADS].set(dt_bias)
    return pl.pallas_call(
        functools.partial(_gdn_kernel, n_prompt_chunks),
        grid=(nchunks,),
        in_specs=[pl.BlockSpec((CHUNK, GDN_CONV_CH), lambda c: (c, C_GQKV // GDN_CONV_CH)),
                  pl.BlockSpec((CHUNK, GDN_WIDTH), lambda c: (c, C_GG // GDN_WIDTH)),
                  pl.BlockSpec((CHUNK, AB_PAD), lambda c: (c, 0)),
                  pl.BlockSpec((CONV_W, GDN_CONV_CH), lambda c: (0, 0)),
                  pl.BlockSpec((1, AB_PAD), lambda c: (0, 0)),
                  pl.BlockSpec((1, AB_PAD), lambda c: (0, 0)),
                  pl.BlockSpec((1, GDN_DV), lambda c: (0, 0)),
                  pl.BlockSpec((1, CONV_W - 1, GDN_CONV_CH), smap3),
                  pl.BlockSpec((1, GDN_HEADS, GDN_DK, GDN_DV), smap4)],
        out_specs=[pl.BlockSpec((CHUNK, GDN_WIDTH), lambda c: (c, 0)),
                   pl.BlockSpec((1, CONV_W - 1, GDN_CONV_CH), smap3),
                   pl.BlockSpec((1, GDN_HEADS, GDN_DK, GDN_DV), smap4)],
        out_shape=[jax.ShapeDtypeStruct((n, GDN_WIDTH), MXU_DTYPE),
                   jax.ShapeDtypeStruct((nstreams, CONV_W - 1, GDN_CONV_CH), F32),
                   jax.ShapeDtypeStruct((nstreams, GDN_HEADS, GDN_DK, GDN_DV), F32)],
        scratch_shapes=[pltpu.VMEM((8 + CHUNK, GDN_CONV_CH), F32),
                        pltpu.VMEM((GDN_HEADS, GDN_DK, GDN_DV), F32)],
        compiler_params=_cparams(("arbitrary",)),
        name="gated_delta",
    )(h, h, ab, conv_w, alog_row, dtb_row, gdn_norm.reshape(1, GDN_DV), hist0, s0)


def _s5_prep_kernel(lre_ref, lim_ref, ldt_ref, btr_ref, bti_ref,
                    bbr_ref, bbi_ref, apr_ref, api_ref):
    lam_re = lre_ref[...]
    lam_im = lim_ref[...]
    dt = jnp.exp(ldt_ref[...])
    mag = jnp.exp(lam_re * dt)
    a_re = mag * jnp.cos(lam_im * dt)
    a_im = mag * jnp.sin(lam_im * dt)
    den = lam_re * lam_re + lam_im * lam_im
    f_re = ((a_re - 1.0) * lam_re + a_im * lam_im) / den
    f_im = (a_im * lam_re - (a_re - 1.0) * lam_im) / den
    btr = btr_ref[...]
    bti = bti_ref[...]
    bbr_ref[...] = f_re * btr - f_im * bti
    bbi_ref[...] = f_re * bti + f_im * btr
    p_re, p_im = a_re, a_im
    apr_ref[0] = p_re
    api_ref[0] = p_im
    for v in range(1, S5_RUN):
        p_re, p_im = p_re * a_re - p_im * a_im, p_re * a_im + p_im * a_re
        apr_ref[v] = p_re
        api_ref[v] = p_im


def _s5_prep(lam_re, lam_im, log_dt, b_re, b_im):
    g, p = lam_re.shape
    nch = b_re.shape[-1]
    return pl.pallas_call(
        _s5_prep_kernel,
        out_shape=[jax.ShapeDtypeStruct((g, nch, p), F32), jax.ShapeDtypeStruct((g, nch, p), F32),
                   jax.ShapeDtypeStruct((S5_RUN, g, 1, p), F32),
                   jax.ShapeDtypeStruct((S5_RUN, g, 1, p), F32)],
        name="s5_discretise",
    )(lam_re.reshape(g, 1, p), lam_im.reshape(g, 1, p), log_dt.reshape(g, 1, 1),
      jnp.swapaxes(b_re, 1, 2), jnp.swapaxes(b_im, 1, 2))


def _s5_kernel(n_prompt_blocks, runs_per_stream, u_ref, bbw_ref, cre_ref, cim_ref, apr_ref, api_ref,
               x0r_ref, x0i_ref, d_ref, z_ref, endr_ref, endi_ref, xr_scr, xi_scr, car_scr, cai_scr):
    b = pl.program_id(1)
    u_perm = jnp.concatenate([u_ref[pl.ds(v, 8, stride=S5_RUN), :] for v in range(S5_RUN)], axis=0)
    e = _dot(_mm(u_perm), bbw_ref[0])
    a_re = jnp.broadcast_to(apr_ref[0:1, :], (8, S5_LB))
    a_im = jnp.broadcast_to(api_ref[0:1, :], (8, S5_LB))

    xr = e[0:8, :S5_LB]
    xi = e[0:8, S5_LB:]
    xr_scr[0:8, :] = xr
    xi_scr[0:8, :] = xi
    for v in range(1, S5_RUN):
        er = e[8 * v:8 * v + 8, :S5_LB]
        ei = e[8 * v:8 * v + 8, S5_LB:]
        xr, xi = a_re * xr - a_im * xi + er, a_re * xi + a_im * xr + ei
        xr_scr[8 * v:8 * v + 8, :] = xr
        xi_scr[8 * v:8 * v + 8, :] = xi

    al_re = apr_ref[S5_RUN - 1:S5_RUN, :]
    al_im = api_ref[S5_RUN - 1:S5_RUN, :]
    x0r = x0r_ref[0]
    x0i = x0i_ref[0]
    is_sample = b >= n_prompt_blocks
    prev_r = car_scr[...]
    prev_i = cai_scr[...]
    cin_r, cin_i, end_r, end_i = [], [], [], []
    for r in range(8):
        if r % runs_per_stream == 0:
            reset = jnp.logical_or(is_sample, b == 0) if r == 0 else is_sample
        else:
            reset = None
        if reset is None:
            cr, ci = prev_r, prev_i
        else:
            cr = jnp.where(reset, x0r[r:r + 1, :], prev_r)
            ci = jnp.where(reset, x0i[r:r + 1, :], prev_i)
        prev_r = al_re * cr - al_im * ci + xr[r:r + 1, :]
        prev_i = al_re * ci + al_im * cr + xi[r:r + 1, :]
        cin_r.append(cr)
        cin_i.append(ci)
        end_r.append(prev_r)
        end_i.append(prev_i)
    car_scr[...] = prev_r
    cai_scr[...] = prev_i
    endr_ref[0] = jnp.concatenate(end_r, axis=0)
    endi_ref[0] = jnp.concatenate(end_i, axis=0)
    cr = jnp.concatenate(cin_r, axis=0)
    ci = jnp.concatenate(cin_i, axis=0)

    for v in range(S5_RUN):
        pr = jnp.broadcast_to(apr_ref[v:v + 1, :], (8, S5_LB))
        pi = jnp.broadcast_to(api_ref[v:v + 1, :], (8, S5_LB))
        xr_scr[8 * v:8 * v + 8, :] = xr_scr[8 * v:8 * v + 8, :] + (pr * cr - pi * ci)
        xi_scr[8 * v:8 * v + 8, :] = xi_scr[8 * v:8 * v + 8, :] + (pr * ci + pi * cr)

    y = _dot(_mm(xr_scr[...]), cre_ref[0]) - _dot(_mm(xi_scr[...]), cim_ref[0])
    z = _gelu_tanh(y + d_ref[...] * u_perm)
    for v in range(S5_RUN):
        z_ref[pl.ds(v, 8, stride=S5_RUN), :] = z[8 * v:8 * v + 8, :]


def _s5_scan(h, bbw, cre, cim, apow_re, apow_im, x0_re, x0_im, d_skip, n_prompt_blocks, runs_per_stream):
    n = h.shape[0]
    nblocks = n // S5_ROWS
    xmap = lambda j, b: (jnp.where(b < n_prompt_blocks, 0, b - n_prompt_blocks + 1), 0, j)
    return pl.pallas_call(
        functools.partial(_s5_kernel, n_prompt_blocks, runs_per_stream),
        grid=(S5_NCB, nblocks),
        in_specs=[pl.BlockSpec((S5_ROWS, S5_CB), lambda j, b: (b, C_SU // S5_CB + j)),
                  pl.BlockSpec((1, S5_CB, 2 * S5_LB), lambda j, b: (j, 0, 0)),
                  pl.BlockSpec((1, S5_LB, S5_CB), lambda j, b: (j, 0, 0)),
                  pl.BlockSpec((1, S5_LB, S5_CB), lambda j, b: (j, 0, 0)),
                  pl.BlockSpec((S5_RUN, S5_LB), lambda j, b: (0, j)),
                  pl.BlockSpec((S5_RUN, S5_LB), lambda j, b: (0, j)),
                  pl.BlockSpec((1, 8, S5_LB), xmap),
                  pl.BlockSpec((1, 8, S5_LB), xmap),
                  pl.BlockSpec((1, S5_CB), lambda j, b: (0, j))],
        out_specs=[pl.BlockSpec((S5_ROWS, S5_CB), lambda j, b: (b, j)),
                   pl.BlockSpec((1, 8, S5_LB), lambda j, b: (b, 0, j)),
                   pl.BlockSpec((1, 8, S5_LB), lambda j, b: (b, 0, j))],
        out_shape=[jax.ShapeDtypeStruct((n, S5_WIDTH), F32),
                   jax.ShapeDtypeStruct((nblocks, 8, S5_LANES), F32),
                   jax.ShapeDtypeStruct((nblocks, 8, S5_LANES), F32)],
        scratch_shapes=[pltpu.VMEM((S5_ROWS, S5_LB), F32), pltpu.VMEM((S5_ROWS, S5_LB), F32),
                        pltpu.VMEM((1, S5_LB), F32), pltpu.VMEM((1, S5_LB), F32)],
        compiler_params=_cparams(("parallel", "arbitrary")),
        name="s5_scan",
    )(h, bbw, cre, cim, apow_re, apow_im, x0_re, x0_im, d_skip.reshape(1, S5_WIDTH))


def _glu_kernel(zr_ref, w_ref, b_ref, zc_ref, o_ref):
    lin = _dot(_mm(zr_ref[...]), w_ref[...]) + b_ref[...]
    o_ref[...] = (zc_ref[...] * _sigmoid(lin)).astype(o_ref.dtype)


def _glu(z, w, bias, tm=512, tn=512):
    n, k = z.shape
    nc = w.shape[1]
    return pl.pallas_call(
        _glu_kernel,
        grid=(n // tm, nc // tn),
        in_specs=[pl.BlockSpec((tm, k), lambda i, j: (i, 0)),
                  pl.BlockSpec((k, tn), lambda i, j: (0, j)),
                  pl.BlockSpec((1, tn), lambda i, j: (0, j)),
                  pl.BlockSpec((tm, tn), lambda i, j: (i, j))],
        out_specs=pl.BlockSpec((tm, tn), lambda i, j: (i, j)),
        out_shape=jax.ShapeDtypeStruct((n, nc), MXU_DTYPE),
        compiler_params=_cparams(("parallel", "parallel")),
        name="s5_glu",
    )(z, w, bias.reshape(1, nc), z)


def _mix_out_kernel(x_ref, a_ref, b_ref, c_ref, w_ref, o_ref):
    ka = a_ref.shape[1]
    kb = b_ref.shape[1]
    acc = _dot(a_ref[...], w_ref[0:ka, :])
    acc = acc + _dot(b_ref[...], w_ref[ka:ka + kb, :])
    acc = acc + _dot(c_ref[...], w_ref[ka + kb:, :])
    o_ref[...] = x_ref[...] + acc


def _mix_out(x, o_ret, o_gdn, o_s5, w, tm=512, tn=1024):
    n, d = x.shape
    k = w.shape[0]
    return pl.pallas_call(
        _mix_out_kernel,
        grid=(n // tm, d // tn),
        in_specs=[pl.BlockSpec((tm, tn), lambda i, j: (i, j)),
                  pl.BlockSpec((tm, o_ret.shape[1]), lambda i, j: (i, 0)),
                  pl.BlockSpec((tm, o_gdn.shape[1]), lambda i, j: (i, 0)),
                  pl.BlockSpec((tm, o_s5.shape[1]), lambda i, j: (i, 0)),
                  pl.BlockSpec((k, tn), lambda i, j: (0, j))],
        out_specs=pl.BlockSpec((tm, tn), lambda i, j: (i, j)),
        out_shape=jax.ShapeDtypeStruct((n, d), F32),
        compiler_params=_cparams(("parallel", "parallel")),
        name="mix_out_proj",
    )(x, o_ret, o_gdn, o_s5, w)


def _peer_hidden_kernel(a_ref, u_ref, o_ref):
    o_ref[...] = _dot_nt(a_ref[...], u_ref[...]).astype(o_ref.dtype)


def _peer_hidden(a, u, tm=512, tn=1024):
    n, k = a.shape
    return pl.pallas_call(
        _peer_hidden_kernel,
        grid=(n // tm, PEER_EXPERTS // tn),
        in_specs=[pl.BlockSpec((tm, k), lambda i, j: (i, 0)),
                  pl.BlockSpec((tn, k), lambda i, j: (j, 0))],
        out_specs=pl.BlockSpec((tm, tn), lambda i, j: (i, j)),
        out_shape=jax.ShapeDtypeStruct((n, PEER_EXPERTS), MXU_DTYPE),
        compiler_params=_cparams(("parallel", "parallel")),
        name="peer_hidden_dense",
    )(a, u)


def _peer_out_kernel(x_ref, c_ref, w_ref, o_ref):
    @pl.when(pl.program_id(2) == 0)
    def _():
        o_ref[...] = x_ref[...]

    o_ref[...] += _dot(c_ref[...], w_ref[...])


def _peer_out(x, c, w, tm=512, tn=1024, tk=4096):
    n, d = x.shape
    return pl.pallas_call(
        _peer_out_kernel,
        grid=(n // tm, d // tn, PEER_EXPERTS // tk),
        in_specs=[pl.BlockSpec((tm, tn), lambda i, j, kk: (i, j)),
                  pl.BlockSpec((tm, tk), lambda i, j, kk: (i, kk)),
                  pl.BlockSpec((tk, tn), lambda i, j, kk: (kk, j))],
        out_specs=pl.BlockSpec((tm, tn), lambda i, j, kk: (i, j)),
        out_shape=jax.ShapeDtypeStruct((n, d), F32),
        compiler_params=_cparams(("parallel", "parallel", "arbitrary")),
        name="peer_out_dense",
    )(x, c, w)


def _final_norm_kernel(n_prompt_tiles, x_ref, g_ref, op_ref, os_ref):
    x = x_ref[...]
    ms = jnp.mean(x * x, axis=-1, keepdims=True)
    y = x * lax.rsqrt(ms + EPS) * g_ref[...]
    i = pl.program_id(0)

    @pl.when(i < n_prompt_tiles)
    def _():
        op_ref[...] = y

    @pl.when(i >= n_prompt_tiles)
    def _():
        os_ref[...] = y


def _final_norm(x, g, n_prompt, tm=512):
    n, d = x.shape
    npt = n_prompt // tm
    return pl.pallas_call(
        functools.partial(_final_norm_kernel, npt),
        grid=(n // tm,),
        in_specs=[pl.BlockSpec((tm, d), lambda i: (i, 0)), pl.BlockSpec((1, d), lambda i: (0, 0))],
        out_specs=[pl.BlockSpec((tm, d), lambda i: (jnp.minimum(i, npt - 1), 0)),
                   pl.BlockSpec((tm, d), lambda i: (jnp.maximum(i - npt, 0), 0))],
        out_shape=[jax.ShapeDtypeStruct((n_prompt, d), F32),
                   jax.ShapeDtypeStruct((n - n_prompt, d), F32)],
        compiler_params=_cparams(("arbitrary",)),
        name="final_norm",
    )(x, g.reshape(1, d))


NEG_INF = float("-inf")


def _top16_rows(s, ids, val_ref, id_ref):
    big = float(2 ** 20)
    for r in range(PEER_TOPK):
        m = jnp.max(s, axis=0, keepdims=True)
        i = jnp.min(jnp.where(s == m, ids, big), axis=0, keepdims=True)
        s = jnp.where(ids == i, NEG_INF, s)
        val_ref[r:r + 1, :] = m
        id_ref[r:r + 1, :] = i
    return val_ref[...], id_ref[...]


def _peer_topk_kernel(q_ref, keys_ref, e1_ref, e2_ref, gate_ref,
                      v0_scr, i0_scr, v1_scr, i1_scr, vb_scr, ib_scr, e1_scr, e2_scr, g_scr):
    tt = q_ref.shape[0]
    key_ids = lax.broadcasted_iota(jnp.int32, (PEER_NKEYS, tt), 0).astype(F32)
    row16 = lax.broadcasted_iota(jnp.int32, (PEER_TOPK, tt), 0).astype(F32)
    row8 = lax.broadcasted_iota(jnp.int32, (8, tt), 0).astype(F32)
    cand_ids = [row16]
    cand_ok = [row16 >= 0.0]
    for i in range(1, 8):
        cand_ids.append(row8 + 16.0 * i)
        cand_ok.append(row8 < float(PEER_TOPK // (i + 1)))
    cand_ids.append(16.0 * (row8 + 8.0))
    cand_ok.append(row8 >= 0.0)
    cand_ids = jnp.concatenate(cand_ids, axis=0)
    cand_ok = jnp.concatenate(cand_ok, axis=0)

    for h in range(PEER_HEADS):
        col = 2 * h * PEER_HALF
        s0 = _dot_nt(keys_ref[2 * h], _mm(q_ref[:, col:col + PEER_HALF]))
        t0, ti0 = _top16_rows(s0, key_ids, v0_scr, i0_scr)
        s1 = _dot_nt(keys_ref[2 * h + 1], _mm(q_ref[:, col + PEER_HALF:col + 2 * PEER_HALF]))
        t1, ti1 = _top16_rows(s1, key_ids, v1_scr, i1_scr)
        cand = [t0[0:1] + t1]
        for i in range(1, 8):
            cand.append(t0[i:i + 1] + t1[0:8])
        cand.append(t0[8:16] + t1[0:1])
        cand = jnp.where(cand_ok, jnp.concatenate(cand, axis=0), NEG_INF)
        best, flat = _top16_rows(cand, cand_ids, vb_scr, ib_scr)
        flat = flat.astype(jnp.int32)
        i_sel = flat >> 4
        j_sel = flat & (PEER_TOPK - 1)
        e1 = jnp.zeros_like(best)
        e2 = jnp.zeros_like(best)
        for r in range(PEER_TOPK):
            e1 = jnp.where(i_sel == r, ti0[r:r + 1], e1)
            e2 = jnp.where(j_sel == r, ti1[r:r + 1], e2)
        p = jnp.exp(best - best[0:1])
        gate = p / jnp.sum(p, axis=0, keepdims=True)
        e1_scr[h * PEER_TOPK:(h + 1) * PEER_TOPK, :] = e1
        e2_scr[h * PEER_TOPK:(h + 1) * PEER_TOPK, :] = e2
        g_scr[h * PEER_TOPK:(h + 1) * PEER_TOPK, :] = gate
    e1_ref[...] = e1_scr[...].T
    e2_ref[...] = e2_scr[...].T
    gate_ref[...] = g_scr[...].T


def _peer_topk(q, keys, tt=256):
    n = q.shape[0]
    spec = pl.BlockSpec((tt, PEER_SLOTS), lambda i: (i, 0))
    rows16 = pltpu.VMEM((PEER_TOPK, tt), F32)
    slots = pltpu.VMEM((PEER_SLOTS, tt), F32)
    return pl.pallas_call(
        _peer_topk_kernel,
        grid=(n // tt,),
        in_specs=[pl.BlockSpec((tt, q.shape[1]), lambda i: (i, 0)),
                  pl.BlockSpec(keys.shape, lambda i: (0, 0, 0))],
        out_specs=[spec, spec, spec],
        out_shape=[jax.ShapeDtypeStruct((n, PEER_SLOTS), F32)] * 3,
        scratch_shapes=[rows16] * 6 + [slots] * 3,
        compiler_params=_cparams(("parallel",)),
        name="peer_topk",
    )(q, keys)


SEL_GROUP = 16
SEL_INTERLEAVE = 8


def _peer_select_kernel(h_ref, e1_ref, e2_ref, gate_ref, c_ref):
    tt = h_ref.shape[0]
    sub = lax.broadcasted_iota(jnp.int32, (PEER_NKEYS, PEER_SLOTS), 0).astype(F32)

    def body(g, carry):
        r0 = pl.multiple_of(g * SEL_GROUP, SEL_GROUP)
        x = h_ref[pl.ds(r0, SEL_GROUP), :]
        x3 = jnp.stack([x[:, e * PEER_NKEYS:(e + 1) * PEER_NKEYS] for e in range(PEER_NKEYS)], axis=0)
        h3 = jnp.swapaxes(x3, 0, 1)
        tiles = []
        for t0 in range(0, SEL_GROUP, SEL_INTERLEAVE):
            toks = range(t0, t0 + SEL_INTERLEAVE)
            on1 = [sub == e1_ref[pl.ds(r0 + t, 1), :] for t in toks]
            b2 = [jnp.where(sub == e2_ref[pl.ds(r0 + t, 1), :], 1.0, 0.0).astype(MXU_DTYPE)
                  for t in toks]
            m = [_dot(h3[t], b) for t, b in zip(toks, b2)]
            hid = [jnp.sum(jnp.where(o, v, 0.0), axis=0, keepdims=True) for o, v in zip(on1, m)]
            coef = [_gelu_tanh(v) * gate_ref[pl.ds(r0 + t, 1), :] for t, v in zip(toks, hid)]
            a1 = [jnp.where(o, c, 0.0).astype(MXU_DTYPE) for o, c in zip(on1, coef)]
            tiles += [_dot_nt(a, b).astype(c_ref.dtype) for a, b in zip(a1, b2)]
        y3 = jnp.swapaxes(jnp.stack(tiles, axis=0), 0, 1)
        for e in range(PEER_NKEYS):
            c_ref[pl.ds(r0, SEL_GROUP), e * PEER_NKEYS:(e + 1) * PEER_NKEYS] = y3[e]
        return carry

    lax.fori_loop(0, tt // SEL_GROUP, body, 0)


def _peer_select(h, e1, e2, gate, tt=64):
    n = h.shape[0]
    spec = pl.BlockSpec((tt, PEER_SLOTS), lambda i: (i, 0))
    dense = pl.BlockSpec((tt, PEER_EXPERTS), lambda i: (i, 0))
    return pl.pallas_call(
        _peer_select_kernel,
        grid=(n // tt,),
        in_specs=[dense, spec, spec, spec],
        out_specs=dense,
        out_shape=jax.ShapeDtypeStruct((n, PEER_EXPERTS), MXU_DTYPE),
        compiler_params=_cparams(("parallel",)),
        name="peer_select_scatter",
    )(h, e1, e2, gate)


def _rope_tables(pos):
    half = RET_DK // 2
    inv = ROPE_BASE ** (-jnp.arange(half, dtype=F32) / half)
    ang = pos.astype(F32)[:, None] * inv[None, :]
    cos = jnp.cos(ang)
    sin = jnp.sin(ang)
    cos_h = jnp.concatenate([cos, cos], axis=-1)
    sin_h = jnp.concatenate([-sin, sin], axis=-1)
    return jnp.tile(cos_h, (1, RET_HEADS)), jnp.tile(sin_h, (1, RET_HEADS))


def _with_zero_stream(state):
    return jnp.concatenate([jnp.zeros_like(state[:1]), state], axis=0)


def _block_diag(x):
    j, g, a, b = x.shape
    eye = jnp.eye(g, dtype=x.dtype)
    return (x[:, :, :, None, :] * eye[None, :, None, :, None]).reshape(j, g * a, g * b)


def kernel(x_prompt, x_sample, state_ret, state_gdn, cache_gdn_conv, state_s5_re, state_s5_im, norm_mix, w_in, w_out, ret_norm, gdn_conv_w, gdn_a_log, gdn_dt_bias, gdn_norm, s5_lam_re, s5_lam_im, s5_log_dt, s5_b_re, s5_b_im, s5_c_re, s5_c_im, s5_d, s5_glu_w, s5_glu_b, norm_ffn, peer_wq, peer_keys, peer_u, peer_v, norm_final):
    bp, tp, d = x_prompt.shape
    nb, ts, _ = x_sample.shape
    depth = w_in.shape[0]
    assert bp == 1 and d == D_MODEL and ts == CHUNK and tp % S5_ROWS == 0
    assert (nb * ts) % S5_ROWS == 0 and CHUNK % S5_RUN == 0
    n_prompt = tp
    n = tp + nb * ts
    n_prompt_chunks = n_prompt // CHUNK
    n_prompt_blocks = n_prompt // S5_ROWS
    runs_per_stream = ts // S5_RUN
    streams_per_block = S5_ROWS // ts

    x = jnp.concatenate([x_prompt.reshape(tp, d), x_sample.reshape(nb * ts, d)], axis=0)
    past_len = 2048
    pos = jnp.concatenate([jnp.arange(tp), jnp.tile(past_len + jnp.arange(ts), nb)])
    cos, sin = _rope_tables(pos)

    ret_out, gdn_out, conv_out, s5r_out, s5i_out = [], [], [], [], []
    for l in range(depth):
        w_l = w_in[l]
        w_main = _mm(jnp.concatenate([w_l[:, :IN_A0], w_l[:, IN_A0 + 2 * GDN_HEADS:]], axis=1))
        w_ab = _mm(jnp.pad(w_l[:, IN_A0:IN_A0 + 2 * GDN_HEADS], ((0, 0), (0, AB_PAD - 2 * GDN_HEADS))))

        h, _, ab = _norm_mm(x, norm_mix[l], w_main, w_ab)

        o_ret, s_ret = _retention(h, cos, sin, _with_zero_stream(state_ret[l]), ret_norm[l], n_prompt_chunks)
        o_gdn, s_conv, s_gdn = _gdn(h, ab, gdn_conv_w[l], gdn_a_log[l], gdn_dt_bias[l], gdn_norm[l],
                                    _with_zero_stream(cache_gdn_conv[l]), _with_zero_stream(state_gdn[l]),
                                    n_prompt_chunks)

        bbt_re, bbt_im, apow_re, apow_im = _s5_prep(s5_lam_re[l], s5_lam_im[l], s5_log_dt[l],
                                                    s5_b_re[l], s5_b_im[l])
        g4 = (S5_NCB, S5_GROUPS // S5_NCB)
        bbw = jnp.concatenate([_block_diag(bbt_re.reshape(*g4, S5_GROUP_CH, S5_STATE)),
                               _block_diag(bbt_im.reshape(*g4, S5_GROUP_CH, S5_STATE))], axis=-1)
        cre = _block_diag(jnp.swapaxes(s5_c_re[l], 1, 2).reshape(*g4, S5_STATE, S5_GROUP_CH))
        cim = _block_diag(jnp.swapaxes(s5_c_im[l], 1, 2).reshape(*g4, S5_STATE, S5_GROUP_CH))

        def run_states(s):
            s = s.reshape(nb // streams_per_block, streams_per_block, 1, S5_LANES)
            s = jnp.broadcast_to(s, (nb // streams_per_block, streams_per_block, runs_per_stream, S5_LANES))
            s = s.reshape(nb // streams_per_block, 8, S5_LANES)
            return jnp.concatenate([jnp.zeros_like(s[:1]), s], axis=0)

        z, end_re, end_im = _s5_scan(h, _mm(bbw), _mm(cre), _mm(cim),
                                     apow_re.reshape(S5_RUN, S5_LANES), apow_im.reshape(S5_RUN, S5_LANES),
                                     run_states(state_s5_re[l]), run_states(state_s5_im[l]), s5_d[l],
                                     n_prompt_blocks, runs_per_stream)
        o_s5 = _glu(z, _mm(s5_glu_w[l]), s5_glu_b[l])

        x = _mix_out(x, o_ret, o_gdn, o_s5, _mm(w_out[l]))

        q, xn, _ = _norm_mm(x, norm_ffn[l], _mm(peer_wq[l]), jnp.zeros((d, AB_PAD), MXU_DTYPE))
        e1, e2, gate = _peer_topk(q, _mm(peer_keys[l].reshape(2 * PEER_HEADS, PEER_NKEYS, PEER_HALF)))
        h3 = _peer_hidden(xn, _mm(peer_u[l]))
        c3 = _peer_select(h3, e1, e2, gate)
        x = _peer_out(x, c3, _mm(peer_v[l]))

        def s5_states(e):
            p = e[n_prompt_blocks - 1, 7].reshape(1, S5_GROUPS, S5_STATE)
            s = e[n_prompt_blocks:].reshape(nb, runs_per_stream, S5_LANES)[:, -1]
            return p, s.reshape(nb, S5_GROUPS, S5_STATE)

        ret_out.append(s_ret)
        gdn_out.append(s_gdn)
        conv_out.append(s_conv)
        s5r_out.append(s5_states(end_re))
        s5i_out.append(s5_states(end_im))

    y_prompt, y_sample = _final_norm(x, norm_final, n_prompt)
    y_prompt = y_prompt.reshape(bp, tp, d)
    y_sample = y_sample.reshape(nb, ts, d)
    split = lambda outs: (jnp.stack([o[:1] for o in outs]), jnp.stack([o[1:] for o in outs]))
    p_ret, s_ret = split(ret_out)
    p_gdn, s_gdn = split(gdn_out)
    p_conv, s_conv = split(conv_out)
    p_s5_re, s_s5_re = (jnp.stack([o[0] for o in s5r_out]), jnp.stack([o[1] for o in s5r_out]))
    p_s5_im, s_s5_im = (jnp.stack([o[0] for o in s5i_out]), jnp.stack([o[1] for o in s5i_out]))
    return (y_prompt, y_sample, p_ret, p_gdn, p_conv, p_s5_re, p_s5_im,
            s_ret, s_gdn, s_conv, s_s5_re, s_s5_im)
```

```python
import functools
import math

import numpy as np
import jax
import jax.numpy as jnp
from jax import lax
from jax.experimental import pallas as pl
from jax.experimental.pallas import tpu as pltpu

F32 = jnp.float32
BF16 = jnp.bfloat16
MXU_DTYPE = BF16
HIGHEST = lax.Precision.HIGHEST

EPS = 1e-6
CHUNK = 64
D_MODEL = 4096
RET_HEADS, RET_DK, RET_DV = 8, 64, 128
RET_QK = RET_HEADS * RET_DK
RET_WIDTH = RET_HEADS * RET_DV
ROPE_BASE = 10000.0
GDN_HEADS, GDN_DK, GDN_DV = 8, 128, 128
GDN_WIDTH = GDN_HEADS * GDN_DV
CONV_W = 4
GDN_CONV_CH = GDN_HEADS * (2 * GDN_DK + GDN_DV)
S5_GROUP_CH, S5_GROUPS, S5_STATE = 16, 128, 64
S5_WIDTH = S5_GROUP_CH * S5_GROUPS
S5_LANES = S5_GROUPS * S5_STATE
PEER_HEADS, PEER_NKEYS, PEER_TOPK = 8, 128, 16
PEER_EXPERTS = PEER_NKEYS * PEER_NKEYS
PEER_HALF = 128
PEER_SLOTS = PEER_HEADS * PEER_TOPK

C_RQ, C_RK, C_RV, C_RG, C_GQKV, C_GG, C_SU = 0, 512, 1024, 2048, 3072, 6144, 7168
H_COLS = 9216
IN_A0 = 2 * RET_QK + 2 * RET_WIDTH + GDN_CONV_CH + GDN_WIDTH
AB_PAD = 128

S5_RUN = 64
S5_ROWS = 8 * S5_RUN
S5_LB = 512
S5_CB = S5_LB // S5_STATE * S5_GROUP_CH
S5_NCB = S5_LANES // S5_LB

VMEM_LIMIT = 56 * 1024 * 1024


def _cparams(sem):
    return pltpu.CompilerParams(dimension_semantics=sem, vmem_limit_bytes=VMEM_LIMIT)


def _dot(a, b, precision=None):
    return jnp.dot(a, b, preferred_element_type=F32, precision=precision)


def _dot_nt(a, b, precision=None):
    return lax.dot_general(a, b, (((1,), (1,)), ((), ())), preferred_element_type=F32,
                           precision=precision)


def _mm(x):
    return x.astype(MXU_DTYPE)


def _sigmoid(x):
    return 1.0 / (1.0 + jnp.exp(-x))


def _silu(x):
    return x * _sigmoid(x)


def _gelu_tanh(x):
    return 0.5 * x * (1.0 + jnp.tanh(math.sqrt(2.0 / math.pi) * (x + 0.044715 * (x * x * x))))


def _norm_mm_kernel(x_ref, g_ref, w_ref, we_ref, o_ref, xn_ref, oe_ref):
    @pl.when(pl.program_id(1) == 0)
    def _():
        x = x_ref[...]
        ms = jnp.mean(x * x, axis=-1, keepdims=True)
        xn = _mm(x * lax.rsqrt(ms + EPS) * g_ref[...])
        xn_ref[...] = xn
        oe_ref[...] = _dot(xn, we_ref[...])

    o_ref[...] = _dot(xn_ref[...], w_ref[...])


def _norm_mm(x, g, w, w_extra, tm=512, tn=1024):
    n, d = x.shape
    nc = w.shape[1]
    ne = w_extra.shape[1]
    return pl.pallas_call(
        _norm_mm_kernel,
        grid=(n // tm, nc // tn),
        in_specs=[pl.BlockSpec((tm, d), lambda i, j: (i, 0)),
                  pl.BlockSpec((1, d), lambda i, j: (0, 0)),
                  pl.BlockSpec((d, tn), lambda i, j: (0, j)),
                  pl.BlockSpec((d, ne), lambda i, j: (0, 0))],
        out_specs=[pl.BlockSpec((tm, tn), lambda i, j: (i, j)),
                   pl.BlockSpec((tm, d), lambda i, j: (i, 0)),
                   pl.BlockSpec((tm, ne), lambda i, j: (i, 0))],
        out_shape=[jax.ShapeDtypeStruct((n, nc), F32),
                   jax.ShapeDtypeStruct((n, d), MXU_DTYPE),
                   jax.ShapeDtypeStruct((n, ne), F32)],
        compiler_params=_cparams(("parallel", "arbitrary")),
        name="norm_proj",
    )(x, g.reshape(1, d), w, w_extra)


def _stream_of_chunk(c, n_prompt_chunks):
    return jnp.where(c < n_prompt_chunks, 0, c - n_prompt_chunks + 1)


def _ret_kernel(n_prompt_chunks, q_ref, k_ref, v_ref, gate_ref, cos_ref, sin_ref, s0_ref, ng_ref,
                o_ref, sout_ref, s_scr):
    c = pl.program_id(0)

    @pl.when(jnp.logical_or(c == 0, c >= n_prompt_chunks))
    def _():
        s_scr[...] = s0_ref[0]

    cos = cos_ref[...]
    sin = sin_ref[...]
    lane = lax.broadcasted_iota(jnp.int32, (CHUNK, RET_QK), 1)
    first_half = (lane % RET_DK) < (RET_DK // 2)

    def rope(x):
        swapped = jnp.where(first_half, pltpu.roll(x, RET_QK - RET_DK // 2, 1),
                            pltpu.roll(x, RET_DK // 2, 1))
        return x * cos + swapped * sin

    q = rope(q_ref[...])
    k = rope(k_ref[...]) * (RET_DK ** -0.5)
    v = v_ref[...]
    gate = gate_ref[...]
    ng = ng_ref[...]

    ti = lax.broadcasted_iota(jnp.int32, (CHUNK, CHUNK), 0)
    tj = lax.broadcasted_iota(jnp.int32, (CHUNK, CHUNK), 1)
    diff = (ti - tj).astype(F32)
    pos = lax.broadcasted_iota(jnp.int32, (CHUNK, 1), 0).astype(F32)

    heads = range(RET_HEADS)
    ld = [math.log(1.0 - 2.0 ** (-5.0 - h)) for h in heads]
    qm = [_mm(q[:, h * RET_DK:(h + 1) * RET_DK]) for h in heads]
    kh = [k[:, h * RET_DK:(h + 1) * RET_DK] for h in heads]
    vm = [_mm(v[:, h * RET_DV:(h + 1) * RET_DV]) for h in heads]
    s = [s_scr[h] for h in heads]
    scores = [_dot_nt(qm[h], _mm(kh[h])) * jnp.where(diff >= 0, jnp.exp(ld[h] * jnp.maximum(diff, 0.0)), 0.0)
              for h in heads]
    cross = [_dot(qm[h], _mm(s[h])) * jnp.exp(ld[h] * (pos + 1.0)) for h in heads]
    o = [_dot(_mm(scores[h]), vm[h]) + cross[h] for h in heads]
    for h in heads:
        kd = kh[h] * jnp.exp(ld[h] * (CHUNK - 1.0 - pos))
        s_scr[h] = s[h] * math.exp(ld[h] * CHUNK) + _dot(_mm(kd.T), vm[h])
    for h in heads:
        mu = jnp.mean(o[h], axis=-1, keepdims=True)
        oc = o[h] - mu
        var = jnp.mean(oc * oc, axis=-1, keepdims=True)
        oh = oc * lax.rsqrt(var + EPS) * ng[:, h * RET_DV:(h + 1) * RET_DV]
        oh = oh * _silu(gate[:, h * RET_DV:(h + 1) * RET_DV])
        o_ref[:, h * RET_DV:(h + 1) * RET_DV] = oh.astype(o_ref.dtype)

    sout_ref[0] = s_scr[...]


def _retention(h, cos, sin, s0, ret_norm, n_prompt_chunks):
    n = h.shape[0]
    nchunks = n // CHUNK
    nstreams = s0.shape[0]
    smap = lambda c: (_stream_of_chunk(c, n_prompt_chunks), 0, 0, 0)
    return pl.pallas_call(
        functools.partial(_ret_kernel, n_prompt_chunks),
        grid=(nchunks,),
        in_specs=[pl.BlockSpec((CHUNK, RET_QK), lambda c: (c, C_RQ // RET_QK)),
                  pl.BlockSpec((CHUNK, RET_QK), lambda c: (c, C_RK // RET_QK)),
                  pl.BlockSpec((CHUNK, RET_WIDTH), lambda c: (c, C_RV // RET_WIDTH)),
                  pl.BlockSpec((CHUNK, RET_WIDTH), lambda c: (c, C_RG // RET_WIDTH)),
                  pl.BlockSpec((CHUNK, RET_QK), lambda c: (c, 0)),
                  pl.BlockSpec((CHUNK, RET_QK), lambda c: (c, 0)),
                  pl.BlockSpec((1, RET_HEADS, RET_DK, RET_DV), smap),
                  pl.BlockSpec((1, RET_WIDTH), lambda c: (0, 0))],
        out_specs=[pl.BlockSpec((CHUNK, RET_WIDTH), lambda c: (c, 0)),
                   pl.BlockSpec((1, RET_HEADS, RET_DK, RET_DV), smap)],
        out_shape=[jax.ShapeDtypeStruct((n, RET_WIDTH), MXU_DTYPE),
                   jax.ShapeDtypeStruct((nstreams, RET_HEADS, RET_DK, RET_DV), F32)],
        scratch_shapes=[pltpu.VMEM((RET_HEADS, RET_DK, RET_DV), F32)],
        compiler_params=_cparams(("arbitrary",)),
        name="retention",
    )(h, h, h, h, cos, sin, s0, ret_norm.reshape(1, RET_WIDTH))


HIST_ROW0 = 8 - (CONV_W - 1)


def _split(x):
    hi = _mm(x)
    return hi, _mm(x - hi.astype(F32))


def _dot_split(a, b):
    return _dot(a[0], b[0]) + (_dot(a[0], b[1]) + _dot(a[1], b[0]))


def _unit_lower_inverses(mats):
    n = mats[0].shape[0]
    ti = lax.broadcasted_iota(jnp.int32, (n, n), 0)
    tj = lax.broadcasted_iota(jnp.int32, (n, n), 1)
    eye = (ti == tj).astype(F32)
    inv = [eye - a for a in mats]
    ps = [_split(a) for a in mats]
    k = 2
    while k < n:
        ps = [_split(_dot_split(x, x)) for x in ps]
        inv = [i + _dot_split(_split(i), x) for i, x in zip(inv, ps)]
        k *= 2
    return inv


def _gdn_kernel(n_prompt_chunks, qkv_ref, gate_ref, ab_ref, cw_ref, alog_ref, dtb_ref, ng_ref,
                hist0_ref, s0_ref, o_ref, hist_out_ref, sout_ref, xin_scr, s_scr):
    c = pl.program_id(0)

    @pl.when(jnp.logical_or(c == 0, c >= n_prompt_chunks))
    def _():
        s_scr[...] = s0_ref[0]
        xin_scr[HIST_ROW0:8, :] = hist0_ref[0]

    xin_scr[8:8 + CHUNK, :] = qkv_ref[...]
    cw = cw_ref[...]
    y = xin_scr[HIST_ROW0:HIST_ROW0 + CHUNK, :] * cw[0:1, :]
    for w in range(1, CONV_W):
        y = y + xin_scr[HIST_ROW0 + w:HIST_ROW0 + w + CHUNK, :] * cw[w:w + 1, :]
    y = _silu(y)
    new_hist = xin_scr[8 + CHUNK - (CONV_W - 1):8 + CHUNK, :]
    hist_out_ref[0] = new_hist
    xin_scr[HIST_ROW0:8, :] = new_hist

    ab = ab_ref[...]
    x = ab + dtb_ref[...]
    softplus = jnp.maximum(x, 0.0) + jnp.log1p(jnp.exp(-jnp.abs(x)))
    g_all = -jnp.exp(alog_ref[...]) * softplus
    beta_all = _sigmoid(ab)

    ti = lax.broadcasted_iota(jnp.int32, (CHUNK, CHUNK), 0)
    tj = lax.broadcasted_iota(jnp.int32, (CHUNK, CHUNK), 1)
    lower = ti >= tj
    strict = ti > tj
    gc_all = _dot(lower.astype(F32), g_all, HIGHEST)
    gc_t = gc_all.T
    gate = gate_ref[...]
    ng = ng_ref[...]

    heads = range(GDN_HEADS)
    gcol = [gc_all[:, h:h + 1] for h in heads]
    decay = [jnp.where(lower, jnp.exp(jnp.where(lower, gcol[h] - gc_t[h:h + 1, :], 0.0)), 0.0)
             for h in heads]
    beta = [beta_all[:, GDN_HEADS + h:GDN_HEADS + h + 1] for h in heads]
    egc = [jnp.exp(g) for g in gcol]
    g_last = [gc_all[CHUNK - 1:CHUNK, h:h + 1] for h in heads]

    def unit(xc):
        return xc * lax.rsqrt(jnp.sum(xc * xc, axis=-1, keepdims=True) + EPS)

    qh = [unit(y[:, h * GDN_DK:(h + 1) * GDN_DK]) * (GDN_DK ** -0.5) for h in heads]
    kh = [unit(y[:, GDN_WIDTH + h * GDN_DK:GDN_WIDTH + (h + 1) * GDN_DK]) for h in heads]
    vc = [y[:, 2 * GDN_WIDTH + h * GDN_DV:2 * GDN_WIDTH + (h + 1) * GDN_DV] for h in heads]
    kb = [kh[h] * beta[h] for h in heads]
    khm = [_mm(k) for k in kh]
    a_mat = [jnp.where(strict, _dot_nt(_mm(kb[h]), khm[h]) * decay[h], 0.0) for h in heads]
    attn = [_mm(_dot_nt(_mm(qh[h]), khm[h]) * decay[h]) for h in heads]
    t_inv = _unit_lower_inverses(a_mat)
    sol = [_dot_split(_split(t_inv[h]),
                      _split(jnp.concatenate([vc[h] * beta[h], kb[h] * egc[h]], axis=-1)))
           for h in heads]
    s = [s_scr[h] for h in heads]
    sm = [_mm(x) for x in s]
    v_new = [sol[h][:, :GDN_DV] - _dot(_mm(sol[h][:, GDN_DV:]), sm[h]) for h in heads]
    vm = [_mm(v) for v in v_new]
    o = [_dot(_mm(qh[h] * egc[h]), sm[h]) + _dot(attn[h], vm[h]) for h in heads]
    for h in heads:
        kd = kh[h] * jnp.exp(g_last[h] - gcol[h])
        s_scr[h] = s[h] * jnp.exp(g_last[h]) + _dot(_mm(kd.T), vm[h])
    for h in heads:
        oh = o[h] * lax.rsqrt(jnp.mean(o[h] * o[h], axis=-1, keepdims=True) + EPS) * ng
        oh = oh * _silu(gate[:, h * GDN_DV:(h + 1) * GDN_DV])
        o_ref[:, h * GDN_DV:(h + 1) * GDN_DV] = oh.astype(o_ref.dtype)

    sout_ref[0] = s_scr[...]


def _gdn(h, ab, conv_w, a_log, dt_bias, gdn_norm, hist0, s0, n_prompt_chunks):
    n = h.shape[0]
    nchunks = n // CHUNK
    nstreams = s0.shape[0]
    smap4 = lambda c: (_stream_of_chunk(c, n_prompt_chunks), 0, 0, 0)
    smap3 = lambda c: (_stream_of_chunk(c, n_prompt_chunks), 0, 0)
    alog_row = jnp.zeros((1, AB_PAD), F32).at[0, :GDN_HEADS].set(a_log)
    dtb_row = jnp.zeros((1, AB_PAD), F32).at[0, :GDN_HEADS].set(dt_bias)
    return pl.pallas_call(
        functools.partial(_gdn_kernel, n_prompt_chunks),
        grid=(nchunks,),
        in_specs=[pl.BlockSpec((CHUNK, GDN_CONV_CH), lambda c: (c, C_GQKV // GDN_CONV_CH)),
                  pl.BlockSpec((CHUNK, GDN_WIDTH), lambda c: (c, C_GG // GDN_WIDTH)),
                  pl.BlockSpec((CHUNK, AB_PAD), lambda c: (c, 0)),
                  pl.BlockSpec((CONV_W, GDN_CONV_CH), lambda c: (0, 0)),
                  pl.BlockSpec((1, AB_PAD), lambda c: (0, 0)),
                  pl.BlockSpec((1, AB_PAD), lambda c: (0, 0)),
                  pl.BlockSpec((1, GDN_DV), lambda c: (0, 0)),
                  pl.BlockSpec((1, CONV_W - 1, GDN_CONV_CH), smap3),
                  pl.BlockSpec((1, GDN_HEADS, GDN_DK, GDN_DV), smap4)],
        out_specs=[pl.BlockSpec((CHUNK, GDN_WIDTH), lambda c: (c, 0)),
                   pl.BlockSpec((1, CONV_W - 1, GDN_CONV_CH), smap3),
                   pl.BlockSpec((1, GDN_HEADS, GDN_DK, GDN_DV), smap4)],
        out_shape=[jax.ShapeDtypeStruct((n, GDN_WIDTH), MXU_DTYPE),
                   jax.ShapeDtypeStruct((nstreams, CONV_W - 1, GDN_CONV_CH), F32),
                   jax.ShapeDtypeStruct((nstreams, GDN_HEADS, GDN_DK, GDN_DV), F32)],
        scratch_shapes=[pltpu.VMEM((8 + CHUNK, GDN_CONV_CH), F32),
                        pltpu.VMEM((GDN_HEADS, GDN_DK, GDN_DV), F32)],
        compiler_params=_cparams(("arbitrary",)),
        name="gated_delta",
    )(h, h, ab, conv_w, alog_row, dtb_row, gdn_norm.reshape(1, GDN_DV), hist0, s0)


def _s5_prep_kernel(lre_ref, lim_ref, ldt_ref, btr_ref, bti_ref,
                    bbr_ref, bbi_ref, apr_ref, api_ref):
    lam_re = lre_ref[...]
    lam_im = lim_ref[...]
    dt = jnp.exp(ldt_ref[...])
    mag = jnp.exp(lam_re * dt)
    a_re = mag * jnp.cos(lam_im * dt)
    a_im = mag * jnp.sin(lam_im * dt)
    den = lam_re * lam_re + lam_im * lam_im
    f_re = ((a_re - 1.0) * lam_re + a_im * lam_im) / den
    f_im = (a_im * lam_re - (a_re - 1.0) * lam_im) / den
    btr = btr_ref[...]
    bti = bti_ref[...]
    bbr_ref[...] = f_re * btr - f_im * bti
    bbi_ref[...] = f_re * bti + f_im * btr
    p_re, p_im = a_re, a_im
    apr_ref[0] = p_re
    api_ref[0] = p_im
    for v in range(1, S5_RUN):
        p_re, p_im = p_re * a_re - p_im * a_im, p_re * a_im + p_im * a_re
        apr_ref[v] = p_re
        api_ref[v] = p_im


def _s5_prep(lam_re, lam_im, log_dt, b_re, b_im):
    g, p = lam_re.shape
    nch = b_re.shape[-1]
    return pl.pallas_call(
        _s5_prep_kernel,
        out_shape=[jax.ShapeDtypeStruct((g, nch, p), F32), jax.ShapeDtypeStruct((g, nch, p), F32),
                   jax.ShapeDtypeStruct((S5_RUN, g, 1, p), F32),
                   jax.ShapeDtypeStruct((S5_RUN, g, 1, p), F32)],
        name="s5_discretise",
    )(lam_re.reshape(g, 1, p), lam_im.reshape(g, 1, p), log_dt.reshape(g, 1, 1),
      jnp.swapaxes(b_re, 1, 2), jnp.swapaxes(b_im, 1, 2))


def _s5_kernel(n_prompt_blocks, runs_per_stream, u_ref, bbw_ref, cre_ref, cim_ref, apr_ref, api_ref,
               x0r_ref, x0i_ref, d_ref, z_ref, endr_ref, endi_ref, xr_scr, xi_scr, car_scr, cai_scr):
    b = pl.program_id(1)
    u_perm = jnp.concatenate([u_ref[pl.ds(v, 8, stride=S5_RUN), :] for v in range(S5_RUN)], axis=0)
    e = _dot(_mm(u_perm), bbw_ref[0])
    a_re = jnp.broadcast_to(apr_ref[0:1, :], (8, S5_LB))
    a_im = jnp.broadcast_to(api_ref[0:1, :], (8, S5_LB))

    xr = e[0:8, :S5_LB]
    xi = e[0:8, S5_LB:]
    xr_scr[0:8, :] = xr
    xi_scr[0:8, :] = xi
    for v in range(1, S5_RUN):
        er = e[8 * v:8 * v + 8, :S5_LB]
        ei = e[8 * v:8 * v + 8, S5_LB:]
        xr, xi = a_re * xr - a_im * xi + er, a_re * xi + a_im * xr + ei
        xr_scr[8 * v:8 * v + 8, :] = xr
        xi_scr[8 * v:8 * v + 8, :] = xi

    al_re = apr_ref[S5_RUN - 1:S5_RUN, :]
    al_im = api_ref[S5_RUN - 1:S5_RUN, :]
    x0r = x0r_ref[0]
    x0i = x0i_ref[0]
    is_sample = b >= n_prompt_blocks
    prev_r = car_scr[...]
    prev_i = cai_scr[...]
    cin_r, cin_i, end_r, end_i = [], [], [], []
    for r in range(8):
        if r % runs_per_stream == 0:
            reset = jnp.logical_or(is_sample, b == 0) if r == 0 else is_sample
        else:
            reset = None
        if reset is None:
            cr, ci = prev_r, prev_i
        else:
            cr = jnp.where(reset, x0r[r:r + 1, :], prev_r)
            ci = jnp.where(reset, x0i[r:r + 1, :], prev_i)
        prev_r = al_re * cr - al_im * ci + xr[r:r + 1, :]
        prev_i = al_re * ci + al_im * cr + xi[r:r + 1, :]
        cin_r.append(cr)
        cin_i.append(ci)
        end_r.append(prev_r)
        end_i.append(prev_i)
    car_scr[...] = prev_r
    cai_scr[...] = prev_i
    endr_ref[0] = jnp.concatenate(end_r, axis=0)
    endi_ref[0] = jnp.concatenate(end_i, axis=0)
    cr = jnp.concatenate(cin_r, axis=0)
    ci = jnp.concatenate(cin_i, axis=0)

    for v in range(S5_RUN):
        pr = jnp.broadcast_to(apr_ref[v:v + 1, :], (8, S5_LB))
        pi = jnp.broadcast_to(api_ref[v:v + 1, :], (8, S5_LB))
        xr_scr[8 * v:8 * v + 8, :] = xr_scr[8 * v:8 * v + 8, :] + (pr * cr - pi * ci)
        xi_scr[8 * v:8 * v + 8, :] = xi_scr[8 * v:8 * v + 8, :] + (pr * ci + pi * cr)

    y = _dot(_mm(xr_scr[...]), cre_ref[0]) - _dot(_mm(xi_scr[...]), cim_ref[0])
    z = _gelu_tanh(y + d_ref[...] * u_perm)
    for v in range(S5_RUN):
        z_ref[pl.ds(v, 8, stride=S5_RUN), :] = z[8 * v:8 * v + 8, :]


def _s5_scan(h, bbw, cre, cim, apow_re, apow_im, x0_re, x0_im, d_skip, n_prompt_blocks, runs_per_stream):
    n = h.shape[0]
    nblocks = n // S5_ROWS
    xmap = lambda j, b: (jnp.where(b < n_prompt_blocks, 0, b - n_prompt_blocks + 1), 0, j)
    return pl.pallas_call(
        functools.partial(_s5_kernel, n_prompt_blocks, runs_per_stream),
        grid=(S5_NCB, nblocks),
        in_specs=[pl.BlockSpec((S5_ROWS, S5_CB), lambda j, b: (b, C_SU // S5_CB + j)),
                  pl.BlockSpec((1, S5_CB, 2 * S5_LB), lambda j, b: (j, 0, 0)),
                  pl.BlockSpec((1, S5_LB, S5_CB), lambda j, b: (j, 0, 0)),
                  pl.BlockSpec((1, S5_LB, S5_CB), lambda j, b: (j, 0, 0)),
                  pl.BlockSpec((S5_RUN, S5_LB), lambda j, b: (0, j)),
                  pl.BlockSpec((S5_RUN, S5_LB), lambda j, b: (0, j)),
                  pl.BlockSpec((1, 8, S5_LB), xmap),
                  pl.BlockSpec((1, 8, S5_LB), xmap),
                  pl.BlockSpec((1, S5_CB), lambda j, b: (0, j))],
        out_specs=[pl.BlockSpec((S5_ROWS, S5_CB), lambda j, b: (b, j)),
                   pl.BlockSpec((1, 8, S5_LB), lambda j, b: (b, 0, j)),
                   pl.BlockSpec((1, 8, S5_LB), lambda j, b: (b, 0, j))],
        out_shape=[jax.ShapeDtypeStruct((n, S5_WIDTH), F32),
                   jax.ShapeDtypeStruct((nblocks, 8, S5_LANES), F32),
                   jax.ShapeDtypeStruct((nblocks, 8, S5_LANES), F32)],
        scratch_shapes=[pltpu.VMEM((S5_ROWS, S5_LB), F32), pltpu.VMEM((S5_ROWS, S5_LB), F32),
                        pltpu.VMEM((1, S5_LB), F32), pltpu.VMEM((1, S5_LB), F32)],
        compiler_params=_cparams(("parallel", "arbitrary")),
        name="s5_scan",
    )(h, bbw, cre, cim, apow_re, apow_im, x0_re, x0_im, d_skip.reshape(1, S5_WIDTH))


def _glu_kernel(zr_ref, w_ref, b_ref, zc_ref, o_ref):
    lin = _dot(_mm(zr_ref[...]), w_ref[...]) + b_ref[...]
    o_ref[...] = (zc_ref[...] * _sigmoid(lin)).astype(o_ref.dtype)


def _glu(z, w, bias, tm=512, tn=512):
    n, k = z.shape
    nc = w.shape[1]
    return pl.pallas_call(
        _glu_kernel,
        grid=(n // tm, nc // tn),
        in_specs=[pl.BlockSpec((tm, k), lambda i, j: (i, 0)),
                  pl.BlockSpec((k, tn), lambda i, j: (0, j)),
                  pl.BlockSpec((1, tn), lambda i, j: (0, j)),
                  pl.BlockSpec((tm, tn), lambda i, j: (i, j))],
        out_specs=pl.BlockSpec((tm, tn), lambda i, j: (i, j)),
        out_shape=jax.ShapeDtypeStruct((n, nc), MXU_DTYPE),
        compiler_params=_cparams(("parallel", "parallel")),
        name="s5_glu",
    )(z, w, bias.reshape(1, nc), z)


def _mix_out_kernel(x_ref, a_ref, b_ref, c_ref, w_ref, o_ref):
    ka = a_ref.shape[1]
    kb = b_ref.shape[1]
    acc = _dot(a_ref[...], w_ref[0:ka, :])
    acc = acc + _dot(b_ref[...], w_ref[ka:ka + kb, :])
    acc = acc + _dot(c_ref[...], w_ref[ka + kb:, :])
    o_ref[...] = x_ref[...] + acc


def _mix_out(x, o_ret, o_gdn, o_s5, w, tm=512, tn=1024):
    n, d = x.shape
    k = w.shape[0]
    return pl.pallas_call(
        _mix_out_kernel,
        grid=(n // tm, d // tn),
        in_specs=[pl.BlockSpec((tm, tn), lambda i, j: (i, j)),
                  pl.BlockSpec((tm, o_ret.shape[1]), lambda i, j: (i, 0)),
                  pl.BlockSpec((tm, o_gdn.shape[1]), lambda i, j: (i, 0)),
                  pl.BlockSpec((tm, o_s5.shape[1]), lambda i, j: (i, 0)),
                  pl.BlockSpec((k, tn), lambda i, j: (0, j))],
        out_specs=pl.BlockSpec((tm, tn), lambda i, j: (i, j)),
        out_shape=jax.ShapeDtypeStruct((n, d), F32),
        compiler_params=_cparams(("parallel", "parallel")),
        name="mix_out_proj",
    )(x, o_ret, o_gdn, o_s5, w)


def _peer_hidden_kernel(a_ref, w_ref, o_ref):
    o_ref[...] = _dot(a_ref[...], w_ref[...]).astype(o_ref.dtype)


def _peer_hidden(a, w, tm=768, tn=1024):
    n, k = a.shape
    return pl.pallas_call(
        _peer_hidden_kernel,
        grid=(n // tm, PEER_EXPERTS // tn),
        in_specs=[pl.BlockSpec((tm, k), lambda i, j: (i, 0)),
                  pl.BlockSpec((k, tn), lambda i, j: (0, j))],
        out_specs=pl.BlockSpec((tm, tn), lambda i, j: (i, j)),
        out_shape=jax.ShapeDtypeStruct((n, PEER_EXPERTS), MXU_DTYPE),
        compiler_params=_cparams(("parallel", "parallel")),
        name="peer_hidden_dense",
    )(a, w)


def _peer_out_kernel(x_ref, c_ref, w_ref, o_ref):
    @pl.when(pl.program_id(2) == 0)
    def _():
        o_ref[...] = x_ref[...]

    o_ref[...] += _dot(c_ref[...], w_ref[...])


def _peer_out(x, c, w, tm=768, tn=1024, tk=4096):
    n, d = x.shape
    return pl.pallas_call(
        _peer_out_kernel,
        grid=(n // tm, d // tn, PEER_EXPERTS // tk),
        in_specs=[pl.BlockSpec((tm, tn), lambda i, j, kk: (i, j)),
                  pl.BlockSpec((tm, tk), lambda i, j, kk: (i, kk)),
                  pl.BlockSpec((tk, tn), lambda i, j, kk: (kk, j))],
        out_specs=pl.BlockSpec((tm, tn), lambda i, j, kk: (i, j)),
        out_shape=jax.ShapeDtypeStruct((n, d), F32),
        compiler_params=_cparams(("parallel", "parallel", "arbitrary")),
        name="peer_out_dense",
    )(x, c, w)


def _final_norm_kernel(n_prompt_tiles, x_ref, g_ref, op_ref, os_ref):
    x = x_ref[...]
    ms = jnp.mean(x * x, axis=-1, keepdims=True)
    y = x * lax.rsqrt(ms + EPS) * g_ref[...]
    i = pl.program_id(0)

    @pl.when(i < n_prompt_tiles)
    def _():
        op_ref[...] = y

    @pl.when(i >= n_prompt_tiles)
    def _():
        os_ref[...] = y


def _final_norm(x, g, n_prompt, tm=512):
    n, d = x.shape
    npt = n_prompt // tm
    return pl.pallas_call(
        functools.partial(_final_norm_kernel, npt),
        grid=(n // tm,),
        in_specs=[pl.BlockSpec((tm, d), lambda i: (i, 0)), pl.BlockSpec((1, d), lambda i: (0, 0))],
        out_specs=[pl.BlockSpec((tm, d), lambda i: (jnp.minimum(i, npt - 1), 0)),
                   pl.BlockSpec((tm, d), lambda i: (jnp.maximum(i - npt, 0), 0))],
        out_shape=[jax.ShapeDtypeStruct((n_prompt, d), F32),
                   jax.ShapeDtypeStruct((n - n_prompt, d), F32)],
        compiler_params=_cparams(("arbitrary",)),
        name="final_norm",
    )(x, g.reshape(1, d))


NEG_INF = float("-inf")


def _top16_rows(s, ids, val_ref, id_ref):
    big = float(2 ** 20)
    for r in range(PEER_TOPK):
        m = jnp.max(s, axis=0, keepdims=True)
        i = jnp.min(jnp.where(s == m, ids, big), axis=0, keepdims=True)
        s = jnp.where(ids == i, NEG_INF, s)
        val_ref[r:r + 1, :] = m
        id_ref[r:r + 1, :] = i
    return val_ref[...], id_ref[...]


def _peer_topk_kernel(q_ref, keys_ref, e1_ref, e2_ref, gate_ref,
                      v0_scr, i0_scr, v1_scr, i1_scr, vb_scr, ib_scr, e1_scr, e2_scr, g_scr):
    tt = q_ref.shape[0]
    key_ids = lax.broadcasted_iota(jnp.int32, (PEER_NKEYS, tt), 0).astype(F32)
    row16 = lax.broadcasted_iota(jnp.int32, (PEER_TOPK, tt), 0).astype(F32)
    row8 = lax.broadcasted_iota(jnp.int32, (8, tt), 0).astype(F32)
    cand_ids = [row16]
    cand_ok = [row16 >= 0.0]
    for i in range(1, 8):
        cand_ids.append(row8 + 16.0 * i)
        cand_ok.append(row8 < float(PEER_TOPK // (i + 1)))
    cand_ids.append(16.0 * (row8 + 8.0))
    cand_ok.append(row8 >= 0.0)
    cand_ids = jnp.concatenate(cand_ids, axis=0)
    cand_ok = jnp.concatenate(cand_ok, axis=0)

    for h in range(PEER_HEADS):
        col = 2 * h * PEER_HALF
        s0 = _dot_nt(keys_ref[2 * h], _mm(q_ref[:, col:col + PEER_HALF]))
        t0, ti0 = _top16_rows(s0, key_ids, v0_scr, i0_scr)
        s1 = _dot_nt(keys_ref[2 * h + 1], _mm(q_ref[:, col + PEER_HALF:col + 2 * PEER_HALF]))
        t1, ti1 = _top16_rows(s1, key_ids, v1_scr, i1_scr)
        cand = [t0[0:1] + t1]
        for i in range(1, 8):
            cand.append(t0[i:i + 1] + t1[0:8])
        cand.append(t0[8:16] + t1[0:1])
        cand = jnp.where(cand_ok, jnp.concatenate(cand, axis=0), NEG_INF)
        best, flat = _top16_rows(cand, cand_ids, vb_scr, ib_scr)
        flat = flat.astype(jnp.int32)
        i_sel = flat >> 4
        j_sel = flat & (PEER_TOPK - 1)
        e1 = jnp.zeros_like(best)
        e2 = jnp.zeros_like(best)
        for r in range(PEER_TOPK):
            e1 = jnp.where(i_sel == r, ti0[r:r + 1], e1)
            e2 = jnp.where(j_sel == r, ti1[r:r + 1], e2)
        p = jnp.exp(best - best[0:1])
        gate = p / jnp.sum(p, axis=0, keepdims=True)
        e1_scr[h * PEER_TOPK:(h + 1) * PEER_TOPK, :] = e1
        e2_scr[h * PEER_TOPK:(h + 1) * PEER_TOPK, :] = e2
        g_scr[h * PEER_TOPK:(h + 1) * PEER_TOPK, :] = gate
    e1_ref[...] = e1_scr[...].T
    e2_ref[...] = e2_scr[...].T
    gate_ref[...] = g_scr[...].T


def _peer_topk(q, keys, tt=256):
    n = q.shape[0]
    spec = pl.BlockSpec((tt, PEER_SLOTS), lambda i: (i, 0))
    rows16 = pltpu.VMEM((PEER_TOPK, tt), F32)
    slots = pltpu.VMEM((PEER_SLOTS, tt), F32)
    return pl.pallas_call(
        _peer_topk_kernel,
        grid=(n // tt,),
        in_specs=[pl.BlockSpec((tt, q.shape[1]), lambda i: (i, 0)),
                  pl.BlockSpec(keys.shape, lambda i: (0, 0, 0))],
        out_specs=[spec, spec, spec],
        out_shape=[jax.ShapeDtypeStruct((n, PEER_SLOTS), F32)] * 3,
        scratch_shapes=[rows16] * 6 + [slots] * 3,
        compiler_params=_cparams(("parallel",)),
        name="peer_topk",
    )(q, keys)


SEL_GROUP = 16
SEL_INTERLEAVE = 8


def _peer_select_kernel(h_ref, e1_ref, e2_ref, gate_ref, c_ref):
    tt = h_ref.shape[0]
    sub = lax.broadcasted_iota(jnp.int32, (PEER_NKEYS, PEER_SLOTS), 0).astype(F32)

    def body(g, carry):
        r0 = pl.multiple_of(g * SEL_GROUP, SEL_GROUP)
        x = h_ref[pl.ds(r0, SEL_GROUP), :]
        x3 = jnp.stack([x[:, e * PEER_NKEYS:(e + 1) * PEER_NKEYS] for e in range(PEER_NKEYS)], axis=0)
        h3 = jnp.swapaxes(x3, 0, 1)
        tiles = []
        for t0 in range(0, SEL_GROUP, SEL_INTERLEAVE):
            toks = range(t0, t0 + SEL_INTERLEAVE)
            on1 = [sub == e1_ref[pl.ds(r0 + t, 1), :] for t in toks]
            b2 = [jnp.where(sub == e2_ref[pl.ds(r0 + t, 1), :], 1.0, 0.0).astype(MXU_DTYPE)
                  for t in toks]
            m = [_dot(h3[t], b) for t, b in zip(toks, b2)]
            hid = [jnp.sum(jnp.where(o, v, 0.0), axis=0, keepdims=True) for o, v in zip(on1, m)]
            coef = [_gelu_tanh(v) * gate_ref[pl.ds(r0 + t, 1), :] for t, v in zip(toks, hid)]
            a1 = [jnp.where(o, c, 0.0).astype(MXU_DTYPE) for o, c in zip(on1, coef)]
            tiles += [_dot_nt(a, b).astype(c_ref.dtype) for a, b in zip(a1, b2)]
        y3 = jnp.swapaxes(jnp.stack(tiles, axis=0), 0, 1)
        for e in range(PEER_NKEYS):
            c_ref[pl.ds(r0, SEL_GROUP), e * PEER_NKEYS:(e + 1) * PEER_NKEYS] = y3[e]
        return carry

    lax.fori_loop(0, tt // SEL_GROUP, body, 0)


def _peer_select(h, e1, e2, gate, tt=128):
    n = h.shape[0]
    spec = pl.BlockSpec((tt, PEER_SLOTS), lambda i: (i, 0))
    dense = pl.BlockSpec((tt, PEER_EXPERTS), lambda i: (i, 0))
    return pl.pallas_call(
        _peer_select_kernel,
        grid=(n // tt,),
        in_specs=[dense, spec, spec, spec],
        out_specs=dense,
        out_shape=jax.ShapeDtypeStruct((n, PEER_EXPERTS), MXU_DTYPE),
        compiler_params=_cparams(("parallel",)),
        name="peer_select_scatter",
    )(h, e1, e2, gate)


def _rope_tables(pos):
    half = RET_DK // 2
    inv = ROPE_BASE ** (-jnp.arange(half, dtype=F32) / half)
    ang = pos.astype(F32)[:, None] * inv[None, :]
    cos = jnp.cos(ang)
    sin = jnp.sin(ang)
    cos_h = jnp.concatenate([cos, cos], axis=-1)
    sin_h = jnp.concatenate([-sin, sin], axis=-1)
    return jnp.tile(cos_h, (1, RET_HEADS)), jnp.tile(sin_h, (1, RET_HEADS))


def _with_zero_stream(state):
    return jnp.concatenate([jnp.zeros_like(state[:1]), state], axis=0)


def _block_diag(x):
    j, g, a, b = x.shape
    eye = jnp.eye(g, dtype=x.dtype)
    return (x[:, :, :, None, :] * eye[None, :, None, :, None]).reshape(j, g * a, g * b)


def kernel(x_prompt, x_sample, state_ret, state_gdn, cache_gdn_conv, state_s5_re, state_s5_im, norm_mix, w_in, w_out, ret_norm, gdn_conv_w, gdn_a_log, gdn_dt_bias, gdn_norm, s5_lam_re, s5_lam_im, s5_log_dt, s5_b_re, s5_b_im, s5_c_re, s5_c_im, s5_d, s5_glu_w, s5_glu_b, norm_ffn, peer_wq, peer_keys, peer_u, peer_v, norm_final):
    bp, tp, d = x_prompt.shape
    nb, ts, _ = x_sample.shape
    depth = w_in.shape[0]
    assert bp == 1 and d == D_MODEL and ts == CHUNK and tp % S5_ROWS == 0
    assert (nb * ts) % S5_ROWS == 0 and CHUNK % S5_RUN == 0
    n_prompt = tp
    n = tp + nb * ts
    n_prompt_chunks = n_prompt // CHUNK
    n_prompt_blocks = n_prompt // S5_ROWS
    runs_per_stream = ts // S5_RUN
    streams_per_block = S5_ROWS // ts

    x = jnp.concatenate([x_prompt.reshape(tp, d), x_sample.reshape(nb * ts, d)], axis=0)
    past_len = 2048
    pos = jnp.concatenate([jnp.arange(tp), jnp.tile(past_len + jnp.arange(ts), nb)])
    cos, sin = _rope_tables(pos)

    ret_out, gdn_out, conv_out, s5r_out, s5i_out = [], [], [], [], []
    for l in range(depth):
        w_l = w_in[l]
        w_main = _mm(jnp.concatenate([w_l[:, :IN_A0], w_l[:, IN_A0 + 2 * GDN_HEADS:]], axis=1))
        w_ab = _mm(jnp.pad(w_l[:, IN_A0:IN_A0 + 2 * GDN_HEADS], ((0, 0), (0, AB_PAD - 2 * GDN_HEADS))))

        h, _, ab = _norm_mm(x, norm_mix[l], w_main, w_ab)

        o_ret, s_ret = _retention(h, cos, sin, _with_zero_stream(state_ret[l]), ret_norm[l], n_prompt_chunks)
        o_gdn, s_conv, s_gdn = _gdn(h, ab, gdn_conv_w[l], gdn_a_log[l], gdn_dt_bias[l], gdn_norm[l],
                                    _with_zero_stream(cache_gdn_conv[l]), _with_zero_stream(state_gdn[l]),
                                    n_prompt_chunks)

        bbt_re, bbt_im, apow_re, apow_im = _s5_prep(s5_lam_re[l], s5_lam_im[l], s5_log_dt[l],
                                                    s5_b_re[l], s5_b_im[l])
        g4 = (S5_NCB, S5_GROUPS // S5_NCB)
        bbw = jnp.concatenate([_block_diag(bbt_re.reshape(*g4, S5_GROUP_CH, S5_STATE)),
                               _block_diag(bbt_im.reshape(*g4, S5_GROUP_CH, S5_STATE))], axis=-1)
        cre = _block_diag(jnp.swapaxes(s5_c_re[l], 1, 2).reshape(*g4, S5_STATE, S5_GROUP_CH))
        cim = _block_diag(jnp.swapaxes(s5_c_im[l], 1, 2).reshape(*g4, S5_STATE, S5_GROUP_CH))

        def run_states(s):
            s = s.reshape(nb // streams_per_block, streams_per_block, 1, S5_LANES)
            s = jnp.broadcast_to(s, (nb // streams_per_block, streams_per_block, runs_per_stream, S5_LANES))
            s = s.reshape(nb // streams_per_block, 8, S5_LANES)
            return jnp.concatenate([jnp.zeros_like(s[:1]), s], axis=0)

        z, end_re, end_im = _s5_scan(h, _mm(bbw), _mm(cre), _mm(cim),
                                     apow_re.reshape(S5_RUN, S5_LANES), apow_im.reshape(S5_RUN, S5_LANES),
                                     run_states(state_s5_re[l]), run_states(state_s5_im[l]), s5_d[l],
                                     n_prompt_blocks, runs_per_stream)
        o_s5 = _glu(z, _mm(s5_glu_w[l]), s5_glu_b[l])

        x = _mix_out(x, o_ret, o_gdn, o_s5, _mm(w_out[l]))

        q, xn, _ = _norm_mm(x, norm_ffn[l], _mm(peer_wq[l]), jnp.zeros((d, AB_PAD), MXU_DTYPE))
        e1, e2, gate = _peer_topk(q, _mm(peer_keys[l].reshape(2 * PEER_HEADS, PEER_NKEYS, PEER_HALF)))
        h3 = _peer_hidden(xn, _mm(peer_u[l].T))
        c3 = _peer_select(h3, e1, e2, gate)
        x = _peer_out(x, c3, _mm(peer_v[l]))

        def s5_states(e):
            p = e[n_prompt_blocks - 1, 7].reshape(1, S5_GROUPS, S5_STATE)
            s = e[n_prompt_blocks:].reshape(nb, runs_per_stream, S5_LANES)[:, -1]
            return p, s.reshape(nb, S5_GROUPS, S5_STATE)

        ret_out.append(s_ret)
        gdn_out.append(s_gdn)
        conv_out.append(s_conv)
        s5r_out.append(s5_states(end_re))
        s5i_out.append(s5_states(end_im))

    y_prompt, y_sample = _final_norm(x, norm_final, n_prompt)
    y_prompt = y_prompt.reshape(bp, tp, d)
    y_sample = y_sample.reshape(nb, ts, d)
    split = lambda outs: (jnp.stack([o[:1] for o in outs]), jnp.stack([o[1:] for o in outs]))
    p_ret, s_ret = split(ret_out)
    p_gdn, s_gdn = split(gdn_out)
    p_conv, s_conv = split(conv_out)
    p_s5_re, s_s5_re = (jnp.stack([o[0] for o in s5r_out]), jnp.stack([o[1] for o in s5r_out]))
    p_s5_im, s_s5_im = (jnp.stack([o[0] for o in s5i_out]), jnp.stack([o[1] for o in s5i_out]))
    return (y_prompt, y_sample, p_ret, p_gdn, p_conv, p_s5_re, p_s5_im,
            s_ret, s_gdn, s_conv, s_s5_re, s_s5_im)
```
